```python
import math
import jax
import jax.numpy as jnp
from jax import lax
import numpy as np

D_MODEL = 2048
BATCH = 2
SEQ = 4096
DEPTH = 4
DEC_BATCH = 128
DEC_SEQ = 4
PAST_LEN = 8192
PAGE_SIZE = 128

N_META = 16
S5_WIDTH = D_MODEL // 2
S5_GROUP = 16
S5_GROUPS = S5_WIDTH // S5_GROUP
S5_STATE = 64
ML_WIDTH = D_MODEL // 2
ML_HEADS = 8
ML_HEAD_DIM = ML_WIDTH // ML_HEADS
ML_CHUNK = 64
ML_FRONT_PAD = (-N_META) % ML_CHUNK
GATE_PAD = -1e9
MIX_IN = S5_WIDTH + 4 * ML_WIDTH + 2 * ML_HEADS
MIX_OUT = S5_WIDTH + ML_WIDTH
MLA_HEADS = 16
Q_LORA = 512
KV_LORA = 512
QK_NOPE = 128
QK_ROPE = 64
V_DIM = 128
ROPE_THETA = 10000.0
Q_BLOCK = 128
MLA_IN = Q_LORA + KV_LORA + QK_ROPE
MLA_SCALE = (QK_NOPE + QK_ROPE) ** -0.5
D_FF = 5632
CONV_W = 3
N_MIX_LAYERS = (DEPTH + 1) // 2
N_MLA_LAYERS = DEPTH // 2
ALPHA = (2 * DEPTH) ** 0.25
BETA = (8 * DEPTH) ** -0.25
LN_EPS = 1e-5
RMS_EPS = 1e-6

kernel_name = 'hybrid_s5_mlstm_mla_convffn_step'


def f32(a):
    return a.astype(jnp.float32)


def layer_norm(x, g, b):
    xf = f32(x)
    mu = xf.mean(-1, keepdims=True)
    var = jnp.square(xf - mu).mean(-1, keepdims=True)
    return ((xf - mu) * lax.rsqrt(var + LN_EPS) * f32(g) + f32(b)).astype(x.dtype)


def rms_norm(x, g):
    xf = f32(x)
    return (xf * lax.rsqrt(jnp.mean(xf * xf, -1, keepdims=True) + RMS_EPS) * f32(g)).astype(x.dtype)


def rope(x, pos):
    half = QK_ROPE // 2
    inv = ROPE_THETA ** (-jnp.arange(half, dtype=jnp.float32) / half)
    ang = pos.astype(jnp.float32)[:, None] * inv[None, :]
    shape = (ang.shape[0],) + (1,) * (x.ndim - 3) + (half,)
    cos, sin = jnp.cos(ang).reshape(shape), jnp.sin(ang).reshape(shape)
    xf = f32(x)
    x1, x2 = xf[..., :half], xf[..., half:]
    return jnp.concatenate([x1 * cos - x2 * sin, x1 * sin + x2 * cos], -1).astype(x.dtype)


def _cplx_affine_combine(e1, e2):
    a1r, a1i, b1r, b1i = e1
    a2r, a2i, b2r, b2i = e2
    return (a2r * a1r - a2i * a1i, a2r * a1i + a2i * a1r,
            a2r * b1r - a2i * b1i + b2r, a2r * b1i + a2i * b1r + b2i)


def s5_scan(u, h_re0, h_im0, a_re, a_im, log_dt, b_re, b_im, c_re, c_im, d, w_glu, b_glu):
    bsz, t, _ = u.shape
    uf = f32(u).reshape(bsz, t, S5_GROUPS, S5_GROUP)
    lam_re, lam_im = f32(a_re), f32(a_im)
    dt = jnp.exp(f32(log_dt))[:, None]
    mag = jnp.exp(lam_re * dt)
    ab_re, ab_im = mag * jnp.cos(lam_im * dt), mag * jnp.sin(lam_im * dt)
    den = lam_re * lam_re + lam_im * lam_im
    z_re = ((ab_re - 1.0) * lam_re + ab_im * lam_im) / den
    z_im = (ab_im * lam_re - (ab_re - 1.0) * lam_im) / den
    br, bi = f32(b_re), f32(b_im)
    bb_re = z_re[..., None] * br - z_im[..., None] * bi
    bb_im = z_re[..., None] * bi + z_im[..., None] * br
    bu_re = jnp.einsum('btgc,gpc->btgp', uf, bb_re)
    bu_im = jnp.einsum('btgc,gpc->btgp', uf, bb_im)
    h0r, h0i = f32(h_re0), f32(h_im0)
    bu_re = bu_re.at[:, 0].add(ab_re * h0r - ab_im * h0i)
    bu_im = bu_im.at[:, 0].add(ab_re * h0i + ab_im * h0r)
    a_re_t = jnp.broadcast_to(ab_re, bu_re.shape)
    a_im_t = jnp.broadcast_to(ab_im, bu_im.shape)
    _, _, h_re, h_im = lax.associative_scan(_cplx_affine_combine, (a_re_t, a_im_t, bu_re, bu_im), axis=1)
    y = (jnp.einsum('btgp,gcp->btgc', h_re, f32(c_re))
         - jnp.einsum('btgp,gcp->btgc', h_im, f32(c_im))
         + f32(d) * uf).reshape(bsz, t, S5_WIDTH)
    z = jax.nn.gelu(y)
    out = z * jax.nn.sigmoid(z @ f32(w_glu) + f32(b_glu))
    return out, h_re[:, -1], h_im[:, -1]


def mlstm_chunk(carry, xs):
    c, n, m = carry
    q, k, v, ig, lf = xs
    L = q.shape[2]
    b = jnp.cumsum(lf, axis=-1)
    causal = jnp.tril(jnp.ones((L, L), dtype=bool))
    log_d = jnp.where(causal, b[..., :, None] - b[..., None, :] + ig[..., None, :], -jnp.inf)
    m_inter = b + m[..., None]
    m_t = jnp.maximum(m_inter, log_d.max(-1))
    dmat = jnp.exp(log_d - m_t[..., None])
    w_inter = jnp.exp(m_inter - m_t)
    s = jnp.einsum('bhtd,bhsd->bhts', q, k) * dmat
    num = w_inter[..., None] * jnp.einsum('bhvk,bhtk->bhtv', c, q) + jnp.einsum('bhts,bhsv->bhtv', s, v)
    den = w_inter * jnp.einsum('bhk,bhtk->bht', n, q) + s.sum(-1)
    h = num / jnp.maximum(jnp.abs(den), jnp.exp(-m_t))[..., None]
    m_new = m_t[..., -1]
    w_s = jnp.exp(b[..., -1:] - b + ig - m_new[..., None])
    f_s = jnp.exp(m_inter[..., -1] - m_new)
    c_new = f_s[..., None, None] * c + jnp.einsum('bhs,bhsv,bhsk->bhvk', w_s, v, k)
    n_new = f_s[..., None] * n + jnp.einsum('bhs,bhsk->bhk', w_s, k)
    return (c_new, n_new, m_new), h


def mlstm_seq(q, k, v, ig, lf, state, front_pad, chunk):
    if front_pad:
        pw = ((0, 0), (front_pad, 0), (0, 0), (0, 0))
        q, k, v = jnp.pad(q, pw), jnp.pad(k, pw), jnp.pad(v, pw)
        ig = jnp.pad(ig, pw[:3], constant_values=GATE_PAD)
        lf = jnp.pad(lf, pw[:3])
    bsz, t = q.shape[:2]
    nc = t // chunk

    def to_chunks(a):
        a = a.reshape((bsz, nc, chunk) + a.shape[2:])
        return jnp.moveaxis(a, (1, 3), (0, 2))

    xs = tuple(to_chunks(a) for a in (q, k, v, ig, lf))
    new_state, hs = lax.scan(mlstm_chunk, state, xs)
    h = jnp.moveaxis(hs, (0, 2), (1, 3)).reshape(bsz, t, ML_HEADS, ML_HEAD_DIM)
    return h[:, front_pad:], new_state


def mixer_ab(x, h_re0, h_im0, c0, n0, m0, front_pad, chunk,
             w_in, b_gates, a_re, a_im, log_dt, b_re, b_im, c_re, c_im, d, w_glu, b_glu, norm_g, w_out):
    bsz, t, _ = x.shape
    proj = x @ w_in
    o1 = S5_WIDTH
    o2, o3, o4, o5 = o1 + ML_WIDTH, o1 + 2 * ML_WIDTH, o1 + 3 * ML_WIDTH, o1 + 4 * ML_WIDTH

    def heads(a):
        return f32(a).reshape(bsz, t, ML_HEADS, ML_HEAD_DIM)

    q = heads(proj[..., o1:o2])
    k = heads(proj[..., o2:o3]) * ML_HEAD_DIM ** -0.5
    v = heads(proj[..., o3:o4])
    og = jax.nn.sigmoid(heads(proj[..., o4:o5]))
    gates = f32(proj[..., o5:]) + f32(b_gates)
    ig = gates[..., :ML_HEADS]
    lf = jax.nn.log_sigmoid(gates[..., ML_HEADS:])
    y_s5, h_re, h_im = s5_scan(proj[..., :o1], h_re0, h_im0, a_re, a_im, log_dt,
                               b_re, b_im, c_re, c_im, d, w_glu, b_glu)
    h, (c1, n1, m1) = mlstm_seq(q, k, v, ig, lf, (f32(c0), f32(n0), f32(m0)), front_pad, chunk)
    mu = h.mean(-1, keepdims=True)
    var = jnp.square(h - mu).mean(-1, keepdims=True)
    hn = (h - mu) * lax.rsqrt(var + LN_EPS) * f32(norm_g).reshape(ML_HEADS, ML_HEAD_DIM)
    y_ml = (og * hn).reshape(bsz, t, ML_WIDTH)
    y = jnp.concatenate([y_s5, y_ml], -1).astype(x.dtype) @ w_out
    return y, h_re, h_im, c1, n1, m1


def mla_project(x, pos, w_in, q_norm_g, kv_norm_g, w_uq):
    bsz, t, _ = x.shape
    proj = x @ w_in
    cq = rms_norm(proj[..., :Q_LORA], q_norm_g)
    ckv = rms_norm(proj[..., Q_LORA:Q_LORA + KV_LORA], kv_norm_g)
    kpe = rope(proj[..., Q_LORA + KV_LORA:], pos)
    q = (cq @ w_uq).reshape(bsz, t, MLA_HEADS, QK_NOPE + QK_ROPE)
    return q[..., :QK_NOPE], rope(q[..., QK_NOPE:], pos), ckv, kpe


def mla_prompt_attend(q_nope, q_pe, ckv, kpe, w_uk, w_uv):
    bsz, t = q_nope.shape[:2]
    k_nope = jnp.einsum('btc,chn->bthn', ckv, w_uk)
    v = jnp.einsum('btc,chv->bthv', ckv, w_uv)
    nb = -(-t // Q_BLOCK)
    tq = nb * Q_BLOCK

    def blocks(a):
        a = jnp.pad(a, ((0, 0), (0, tq - t), (0, 0), (0, 0)))
        return jnp.moveaxis(a.reshape((bsz, nb, Q_BLOCK) + a.shape[2:]), 1, 0)

    qpos = jnp.arange(tq).reshape(nb, Q_BLOCK)
    kpos = jnp.arange(t)

    def block(args):
        qn_b, qp_b, qpos_b = args
        s = jnp.einsum('bqhn,bkhn->bhqk', qn_b, k_nope) + jnp.einsum('bqhr,bkr->bhqk', qp_b, kpe)
        s = jnp.where(kpos[None, :] <= qpos_b[:, None], f32(s) * MLA_SCALE, -jnp.inf)
        p = jax.nn.softmax(s, axis=-1).astype(v.dtype)
        return jnp.einsum('bhqk,bkhv->bqhv', p, v)

    o = lax.map(block, (blocks(q_nope), blocks(q_pe), qpos))
    return jnp.moveaxis(o, 0, 1).reshape(bsz, tq, MLA_HEADS, V_DIM)[:, :t]


def mla_sample_attend(q_nope, q_pe, ckv_new, kpe_new, ckv_past, kpe_past, w_uk, w_uv):
    tn = q_nope.shape[1]
    past = ckv_past.shape[1]
    q_lat = jnp.einsum('bqhn,chn->bqhc', q_nope, w_uk)
    s_past = jnp.einsum('bqhc,btc->bhqt', q_lat, ckv_past) + jnp.einsum('bqhr,btr->bhqt', q_pe, kpe_past)
    s_new = jnp.einsum('bqhc,bjc->bhqj', q_lat, ckv_new) + jnp.einsum('bqhr,bjr->bhqj', q_pe, kpe_new)
    causal = jnp.tril(jnp.ones((tn, tn), dtype=bool))
    s = jnp.concatenate([f32(s_past) * MLA_SCALE,
                         jnp.where(causal, f32(s_new) * MLA_SCALE, -jnp.inf)], -1)
    p = jax.nn.softmax(s, axis=-1).astype(ckv_past.dtype)
    o_lat = (jnp.einsum('bhqt,btc->bqhc', p[..., :past], ckv_past)
             + jnp.einsum('bhqj,bjc->bqhc', p[..., past:], ckv_new))
    return jnp.einsum('bqhc,chv->bqhv', o_lat, w_uv)


def conv_ffn(x, prev, w_up, conv_w, conv_b, w_down):
    t = x.shape[1]
    up = x @ w_up
    ext = jnp.concatenate([prev.astype(up.dtype), up], 1)
    h = conv_b
    for j in range(CONV_W):
        h = h + ext[:, j:j + t] * conv_w[j]
    y = (jax.nn.silu(h[..., D_FF:]) * h[..., :D_FF]) @ w_down
    return y, ext[:, -(CONV_W - 1):]


def setup_inputs(seed: int = 0) -> dict:
    key = jax.random.key(seed)
    ks = iter(jax.random.split(key, 64))

    def nrm(shape, scale=1.0):
        return jax.random.normal(next(ks), shape, jnp.float32) * scale

    n_pages = PAST_LEN // PAGE_SIZE
    n_used = DEC_BATCH * n_pages
    n_pool = n_used + max(1, n_used // 4)
    nm, na = N_MIX_LAYERS, N_MLA_LAYERS
    x_prompt = nrm((BATCH, SEQ, D_MODEL))
    x_sample = nrm((DEC_BATCH, DEC_SEQ, D_MODEL))
    cache_mla_ckv = nrm((na, n_pool, PAGE_SIZE, KV_LORA))
    cache_mla_kpe = nrm((na, n_pool, PAGE_SIZE, QK_ROPE))
    page_table = jax.random.permutation(next(ks), n_pool)[:n_used].reshape(DEC_BATCH, n_pages).astype(jnp.int32)
    state_s5_re = nrm((nm, DEC_BATCH, S5_GROUPS, S5_STATE), 0.3)
    state_s5_im = nrm((nm, DEC_BATCH, S5_GROUPS, S5_STATE), 0.3)
    state_mlstm_c = nrm((nm, DEC_BATCH, ML_HEADS, ML_HEAD_DIM, ML_HEAD_DIM), 0.05)
    state_mlstm_n = nrm((nm, DEC_BATCH, ML_HEADS, ML_HEAD_DIM), 0.05)
    state_mlstm_m = jax.random.uniform(next(ks), (nm, DEC_BATCH, ML_HEADS), jnp.float32, 0.0, 3.0)
    state_ffn_conv = nrm((DEPTH, DEC_BATCH, CONV_W - 1, 2 * D_FF))
    meta_tokens = nrm((N_META, D_MODEL))
    ln1_g = 1.0 + nrm((DEPTH, D_MODEL), 0.02)
    ln1_b = nrm((DEPTH, D_MODEL), 0.02)
    ln2_g = 1.0 + nrm((DEPTH, D_MODEL), 0.02)
    ln2_b = nrm((DEPTH, D_MODEL), 0.02)
    mix_w_in = nrm((nm, D_MODEL, MIX_IN), D_MODEL ** -0.5)
    f_bias = jnp.linspace(3.0, 6.0, ML_HEADS, dtype=jnp.float32)
    mix_b_gates = jnp.concatenate([nrm((nm, ML_HEADS), 0.1), f_bias + nrm((nm, ML_HEADS), 0.1)], -1)
    s5_a_re = -0.5 + nrm((nm, S5_GROUPS, S5_STATE), 0.01)
    s5_a_im = math.pi * jnp.arange(S5_STATE, dtype=jnp.float32) + nrm((nm, S5_GROUPS, S5_STATE), 0.01)
    s5_log_dt = jax.random.uniform(next(ks), (nm, S5_GROUPS), jnp.float32, math.log(1e-3), math.log(1e-1))
    s5_b_re = nrm((nm, S5_GROUPS, S5_STATE, S5_GROUP), S5_GROUP ** -0.5)
    s5_b_im = nrm((nm, S5_GROUPS, S5_STATE, S5_GROUP), S5_GROUP ** -0.5)
    s5_c_re = nrm((nm, S5_GROUPS, S5_GROUP, S5_STATE), 0.5)
    s5_c_im = nrm((nm, S5_GROUPS, S5_GROUP, S5_STATE), 0.5)
    s5_d = nrm((nm, S5_GROUPS, S5_GROUP))
    s5_w_glu = nrm((nm, S5_WIDTH, S5_WIDTH), S5_WIDTH ** -0.5)
    s5_b_glu = nrm((nm, S5_WIDTH), 0.02)
    ml_norm_g = 1.0 + nrm((nm, ML_WIDTH), 0.02)
    mix_w_out = nrm((nm, MIX_OUT, D_MODEL), BETA * MIX_OUT ** -0.5)
    mla_w_in = nrm((na, D_MODEL, MLA_IN), D_MODEL ** -0.5)
    mla_q_norm_g = 1.0 + nrm((na, Q_LORA), 0.02)
    mla_kv_norm_g = 1.0 + nrm((na, KV_LORA), 0.02)
    mla_w_uq = nrm((na, Q_LORA, MLA_HEADS * (QK_NOPE + QK_ROPE)), Q_LORA ** -0.5)
    mla_w_uk = nrm((na, KV_LORA, MLA_HEADS, QK_NOPE), KV_LORA ** -0.5)
    mla_w_uv = nrm((na, KV_LORA, MLA_HEADS, V_DIM), KV_LORA ** -0.5)
    mla_w_out = nrm((na, MLA_HEADS * V_DIM, D_MODEL), BETA * (MLA_HEADS * V_DIM) ** -0.5)
    ffn_w_up = nrm((DEPTH, D_MODEL, 2 * D_FF), D_MODEL ** -0.5)
    ffn_conv_w = nrm((DEPTH, CONV_W, 2 * D_FF), CONV_W ** -0.5)
    ffn_conv_b = nrm((DEPTH, 2 * D_FF), 0.02)
    ffn_w_down = nrm((DEPTH, D_FF, D_MODEL), BETA * D_FF ** -0.5)
    return {
        'x_prompt': x_prompt, 'x_sample': x_sample,
        'cache_mla_ckv': cache_mla_ckv, 'cache_mla_kpe': cache_mla_kpe, 'page_table': page_table,
        'state_s5_re': state_s5_re, 'state_s5_im': state_s5_im,
        'state_mlstm_c': state_mlstm_c, 'state_mlstm_n': state_mlstm_n, 'state_mlstm_m': state_mlstm_m,
        'state_ffn_conv': state_ffn_conv,
        'meta_tokens': meta_tokens,
        'ln1_g': ln1_g, 'ln1_b': ln1_b, 'ln2_g': ln2_g, 'ln2_b': ln2_b,
        'mix_w_in': mix_w_in, 'mix_b_gates': mix_b_gates,
        's5_a_re': s5_a_re, 's5_a_im': s5_a_im, 's5_log_dt': s5_log_dt,
        's5_b_re': s5_b_re, 's5_b_im': s5_b_im, 's5_c_re': s5_c_re, 's5_c_im': s5_c_im, 's5_d': s5_d,
        's5_w_glu': s5_w_glu, 's5_b_glu': s5_b_glu, 'ml_norm_g': ml_norm_g, 'mix_w_out': mix_w_out,
        'mla_w_in': mla_w_in, 'mla_q_norm_g': mla_q_norm_g, 'mla_kv_norm_g': mla_kv_norm_g,
        'mla_w_uq': mla_w_uq, 'mla_w_uk': mla_w_uk, 'mla_w_uv': mla_w_uv, 'mla_w_out': mla_w_out,
        'ffn_w_up': ffn_w_up, 'ffn_conv_w': ffn_conv_w, 'ffn_conv_b': ffn_conv_b, 'ffn_w_down': ffn_w_down,
    }


def reference(x_prompt, x_sample, cache_mla_ckv, cache_mla_kpe, page_table,
              state_s5_re, state_s5_im, state_mlstm_c, state_mlstm_n, state_mlstm_m, state_ffn_conv,
              meta_tokens, ln1_g, ln1_b, ln2_g, ln2_b,
              mix_w_in, mix_b_gates, s5_a_re, s5_a_im, s5_log_dt, s5_b_re, s5_b_im, s5_c_re, s5_c_im, s5_d,
              s5_w_glu, s5_b_glu, ml_norm_g, mix_w_out,
              mla_w_in, mla_q_norm_g, mla_kv_norm_g, mla_w_uq, mla_w_uk, mla_w_uv, mla_w_out,
              ffn_w_up, ffn_conv_w, ffn_conv_b, ffn_w_down):
    dt_act = x_prompt.dtype
    bp = x_prompt.shape[0]
    bs, ts = x_sample.shape[:2]
    past = page_table.shape[1] * cache_mla_ckv.shape[2]
    xp = jnp.concatenate([jnp.broadcast_to(meta_tokens.astype(dt_act)[None], (bp, N_META, D_MODEL)), x_prompt], 1)
    tp = xp.shape[1]
    xs = x_sample
    pos_p = jnp.arange(tp)
    pos_s = past + jnp.arange(ts)

    ckv_p_l, kpe_p_l, ckv_s_l, kpe_s_l = [], [], [], []
    s5r_p_l, s5i_p_l, s5r_s_l, s5i_s_l = [], [], [], []
    mc_p_l, mn_p_l, mm_p_l, mc_s_l, mn_s_l, mm_s_l = [], [], [], [], [], []
    conv_p_l, conv_s_l = [], []

    for l in range(DEPTH):
        j = l // 2
        if l % 2 == 0:
            lp = (mix_w_in[j], mix_b_gates[j], s5_a_re[j], s5_a_im[j], s5_log_dt[j], s5_b_re[j], s5_b_im[j],
                  s5_c_re[j], s5_c_im[j], s5_d[j], s5_w_glu[j], s5_b_glu[j], ml_norm_g[j], mix_w_out[j])
            z_s5 = jnp.zeros((bp, S5_GROUPS, S5_STATE), dt_act)
            z_c = jnp.zeros((bp, ML_HEADS, ML_HEAD_DIM, ML_HEAD_DIM), dt_act)
            z_n = jnp.zeros((bp, ML_HEADS, ML_HEAD_DIM), dt_act)
            z_m = jnp.zeros((bp, ML_HEADS), dt_act)
            yp, hr, hi, c1, n1, m1 = mixer_ab(xp, z_s5, z_s5, z_c, z_n, z_m, ML_FRONT_PAD, ML_CHUNK, *lp)
            s5r_p_l.append(hr); s5i_p_l.append(hi); mc_p_l.append(c1); mn_p_l.append(n1); mm_p_l.append(m1)
            ys, hr, hi, c1, n1, m1 = mixer_ab(xs, state_s5_re[j], state_s5_im[j], state_mlstm_c[j],
                                              state_mlstm_n[j], state_mlstm_m[j], 0, ts, *lp)
            s5r_s_l.append(hr); s5i_s_l.append(hi); mc_s_l.append(c1); mn_s_l.append(n1); mm_s_l.append(m1)
        else:
            mp = (mla_w_in[j], mla_q_norm_g[j], mla_kv_norm_g[j], mla_w_uq[j])
            qn, qpe, ckv, kpe = mla_project(xp, pos_p, *mp)
            o = mla_prompt_attend(qn, qpe, ckv, kpe, mla_w_uk[j], mla_w_uv[j])
            yp = o.reshape(bp, tp, MLA_HEADS * V_DIM) @ mla_w_out[j]
            ckv_p_l.append(ckv); kpe_p_l.append(kpe)
            qn, qpe, ckv, kpe = mla_project(xs, pos_s, *mp)
            ckv_past = cache_mla_ckv[j, page_table].reshape(bs, past, KV_LORA)
            kpe_past = cache_mla_kpe[j, page_table].reshape(bs, past, QK_ROPE)
            o = mla_sample_attend(qn, qpe, ckv, kpe, ckv_past, kpe_past, mla_w_uk[j], mla_w_uv[j])
            ys = o.reshape(bs, ts, MLA_HEADS * V_DIM) @ mla_w_out[j]
            ckv_s_l.append(ckv); kpe_s_l.append(kpe)
        xp = layer_norm(ALPHA * xp + yp, ln1_g[l], ln1_b[l])
        xs = layer_norm(ALPHA * xs + ys, ln1_g[l], ln1_b[l])
        fp, cp = conv_ffn(xp, jnp.zeros((bp, CONV_W - 1, 2 * D_FF), dt_act),
                          ffn_w_up[l], ffn_conv_w[l], ffn_conv_b[l], ffn_w_down[l])
        fs, cs = conv_ffn(xs, state_ffn_conv[l], ffn_w_up[l], ffn_conv_w[l], ffn_conv_b[l], ffn_w_down[l])
        conv_p_l.append(cp); conv_s_l.append(cs)
        xp = layer_norm(ALPHA * xp + fp, ln2_g[l], ln2_b[l])
        xs = layer_norm(ALPHA * xs + fs, ln2_g[l], ln2_b[l])

    def st(lst):
        return jnp.stack(lst).astype(dt_act)

    y_prompt = xp[:, N_META:]
    y_sample = xs
    return (y_prompt, y_sample,
            st(ckv_p_l), st(kpe_p_l), st(ckv_s_l), st(kpe_s_l),
            st(s5r_p_l), st(s5i_p_l), st(s5r_s_l), st(s5i_s_l),
            st(mc_p_l), st(mn_p_l), st(mm_p_l), st(mc_s_l), st(mn_s_l), st(mm_s_l),
            st(conv_p_l), st(conv_s_l))
```

```python
import functools
import math

import jax
import jax.numpy as jnp
from jax import lax
from jax.experimental import pallas as pl
from jax.experimental.pallas import tpu as pltpu

BF = jnp.bfloat16
F32 = jnp.float32

D_MODEL = 2048
BATCH = 2
SEQ = 4096
DEPTH = 4
DEC_BATCH = 128
DEC_SEQ = 4
N_META = 16
S5_WIDTH = 1024
S5_GROUP = 16
S5_GROUPS = 64
S5_STATE = 64
ML_WIDTH = 1024
ML_HEADS = 8
ML_HEAD_DIM = 128
MLA_HEADS = 16
Q_LORA = 512
KV_LORA = 512
QK_NOPE = 128
QK_ROPE = 64
V_DIM = 128
ROPE_THETA = 10000.0
MLA_SCALE = (QK_NOPE + QK_ROPE) ** -0.5
D_FF = 5632
ALPHA = (2 * DEPTH) ** 0.25
LN_EPS = 1e-5
RMS_EPS = 1e-6
NEG = -1e30

LANES = 128
SUBLANES = 8
BF16_ROWS = 16
VMEM_LIMIT = 56 * 1024 * 1024

T_REAL = N_META + SEQ
T_PAD = 4224
ROWS_P = BATCH * T_PAD
ROWS_S = DEC_BATCH * DEC_SEQ
S5_BLK = 512
N_S5_BLK = S5_GROUPS * S5_STATE // S5_BLK
PAGES_PER_STEP = 16


def _cp(sem, vmem=VMEM_LIMIT):
    return pltpu.CompilerParams(dimension_semantics=sem, vmem_limit_bytes=vmem)


def _dot(a, b):
    return jnp.dot(a, b, preferred_element_type=F32)


def _dot_nt(a, b):
    return lax.dot_general(a, b, (((1,), (1,)), ((), ())), preferred_element_type=F32)


def _sigmoid(x):
    return 1.0 / (1.0 + jnp.exp(-x))


def _layer_norm(xf, g, b):
    mu = jnp.mean(xf, axis=-1, keepdims=True)
    xc = xf - mu
    var = jnp.mean(xc * xc, axis=-1, keepdims=True)
    return xc * lax.rsqrt(var + LN_EPS) * g + b


def _mm_body(x_ref, w_ref, o_ref):
    o_ref[...] = _dot(x_ref[...].astype(BF), w_ref[...]).astype(o_ref.dtype)


def _mm(x, w, layer, out_dtype, tm, tn):
    rows, k = x.shape
    n = w.shape[-1]
    return pl.pallas_call(
        _mm_body,
        grid=(rows // tm, n // tn),
        in_specs=[pl.BlockSpec((tm, k), lambda i, j: (i, 0)),
                  pl.BlockSpec((None, k, tn), lambda i, j: (layer, 0, j))],
        out_specs=pl.BlockSpec((tm, tn), lambda i, j: (i, j)),
        out_shape=jax.ShapeDtypeStruct((rows, n), out_dtype),
        compiler_params=_cp(("parallel", "arbitrary")),
        name="mm",
    )(x, w)


def _mix_proj_body(x_ref, w_ref, wg_ref, p_ref, g_ref, xb_s):
    @pl.when(pl.program_id(1) == 0)
    def _():
        xb = x_ref[...].astype(BF)
        xb_s[...] = xb
        g_ref[...] = _dot(xb, wg_ref[...])

    p_ref[...] = _dot(xb_s[...], w_ref[...])


def _mix_proj(x, w_in_b, w_g_b, layer, tm):
    rows = x.shape[0]
    tn = 512
    n_main = S5_WIDTH + 4 * ML_WIDTH
    return pl.pallas_call(
        _mix_proj_body,
        grid=(rows // tm, n_main // tn),
        in_specs=[pl.BlockSpec((tm, D_MODEL), lambda i, j: (i, 0)),
                  pl.BlockSpec((None, D_MODEL, tn), lambda i, j: (layer, 0, j)),
                  pl.BlockSpec((None, D_MODEL, LANES), lambda i, j: (layer, 0, 0))],
        out_specs=[pl.BlockSpec((tm, tn), lambda i, j: (i, j)),
                   pl.BlockSpec((tm, LANES), lambda i, j: (i, 0))],
        out_shape=[jax.ShapeDtypeStruct((rows, n_main), F32),
                   jax.ShapeDtypeStruct((rows, LANES), F32)],
        scratch_shapes=[pltpu.VMEM((tm, D_MODEL), BF)],
        compiler_params=_cp(("parallel", "arbitrary")),
        name="mix_proj",
    )(x, w_in_b, w_g_b)


def _s5_params_body(are_ref, aim_ref, ldt_ref, bre_ref, bim_ref, abr_ref, abi_ref, bbr_ref, bbi_ref):
    lam_re = are_ref[...]
    lam_im = aim_ref[...]
    dt = jnp.exp(ldt_ref[...])
    mag = jnp.exp(lam_re * dt)
    ab_re = mag * jnp.cos(lam_im * dt)
    ab_im = mag * jnp.sin(lam_im * dt)
    den = lam_re * lam_re + lam_im * lam_im
    z_re = ((ab_re - 1.0) * lam_re + ab_im * lam_im) / den
    z_im = (ab_im * lam_re - (ab_re - 1.0) * lam_im) / den
    br = bre_ref[...]
    bi = bim_ref[...]
    abr_ref[...] = ab_re
    abi_ref[...] = ab_im
    bbr_ref[...] = z_re * br - z_im * bi
    bbi_ref[...] = z_re * bi + z_im * br


def _s5_params(a_re, a_im, log_dt, b_re, b_im):
    n = S5_GROUPS * S5_STATE
    col = lambda a: a.reshape(n, 1)
    ldt = jnp.broadcast_to(log_dt[:, None], (S5_GROUPS, S5_STATE)).reshape(n, 1)
    out = pl.pallas_call(
        _s5_params_body,
        out_shape=[jax.ShapeDtypeStruct((n, 1), F32), jax.ShapeDtypeStruct((n, 1), F32),
                   jax.ShapeDtypeStruct((n, S5_GROUP), F32), jax.ShapeDtypeStruct((n, S5_GROUP), F32)],
        name="s5_params",
    )(col(a_re), col(a_im), ldt, b_re.reshape(n, S5_GROUP), b_im.reshape(n, S5_GROUP))
    ab_re, ab_im, bb_re, bb_im = out
    return ab_re.reshape(1, n), ab_im.reshape(1, n), bb_re, bb_im


def _blockdiag_in(bb):
    nb = N_S5_BLK
    g = S5_GROUPS // nb
    b4 = bb.reshape(nb, g, S5_STATE, S5_GROUP).transpose(0, 1, 3, 2)
    eye = jnp.eye(g, dtype=bool)[None, :, None, :, None]
    out = jnp.where(eye, b4[:, :, :, None, :], 0.0)
    return out.reshape(nb, g * S5_GROUP, g * S5_STATE).astype(BF)


def _blockdiag_out(c):
    nb = N_S5_BLK
    g = S5_GROUPS // nb
    c4 = c.reshape(nb, g, S5_GROUP, S5_STATE).transpose(0, 1, 3, 2)
    eye = jnp.eye(g, dtype=bool)[None, :, None, :, None]
    out = jnp.where(eye, c4[:, :, :, None, :], 0.0)
    return out.reshape(nb, g * S5_STATE, g * S5_GROUP).astype(BF)


def _cmul(ar, ai, br, bi):
    return ar * br - ai * bi, ar * bi + ai * br


def _s5_prompt_body(u_ref, bre_ref, bim_ref, cre_ref, cim_ref, d_ref, ar_ref, ai_ref,
                    y_ref, hre_ref, him_ref, xr_s, xi_s, car_s, cai_s, *, tc, t_final):
    c = pl.program_id(2)

    @pl.when(c == 0)
    def _():
        car_s[...] = jnp.zeros_like(car_s)
        cai_s[...] = jnp.zeros_like(cai_s)

    u = u_ref[...]
    ub = u.astype(BF)
    xr_s[...] = _dot(ub, bre_ref[...])
    xi_s[...] = _dot(ub, bim_ref[...])

    a1 = (ar_ref[...], ai_ref[...])
    a2 = _cmul(*a1, *a1)
    a3 = _cmul(*a2, *a1)
    a4 = _cmul(*a2, *a2)
    a5 = _cmul(*a4, *a1)
    a6 = _cmul(*a4, *a2)
    a7 = _cmul(*a4, *a3)
    a8 = _cmul(*a4, *a4)
    pows = (a1, a2, a3, a4, a5, a6, a7, a8)
    pw_r = jnp.concatenate([p[0] for p in pows], axis=0)
    pw_i = jnp.concatenate([p[1] for p in pows], axis=0)
    sub = lax.broadcasted_iota(jnp.int32, (SUBLANES, S5_BLK), 0)

    def tile(n, carry):
        cr, ci = carry
        off = pl.multiple_of(n * SUBLANES, SUBLANES)
        xr = xr_s[pl.ds(off, SUBLANES), :]
        xi = xi_s[pl.ds(off, SUBLANES), :]
        for d, (pr, pi) in ((1, a1), (2, a2), (4, a4)):
            keep = sub >= d
            sr = jnp.where(keep, pltpu.roll(xr, d, 0), 0.0)
            si = jnp.where(keep, pltpu.roll(xi, d, 0), 0.0)
            xr, xi = xr + pr * sr - pi * si, xi + pr * si + pi * sr
        hr = xr + pw_r * cr - pw_i * ci
        hi = xi + pw_r * ci + pw_i * cr
        xr_s[pl.ds(off, SUBLANES), :] = hr
        xi_s[pl.ds(off, SUBLANES), :] = hi
        return hr[SUBLANES - 1:SUBLANES, :], hi[SUBLANES - 1:SUBLANES, :]

    cr, ci = lax.fori_loop(0, tc // SUBLANES, tile, (car_s[...], cai_s[...]), unroll=4)
    car_s[...] = cr
    cai_s[...] = ci

    hr = xr_s[...]
    hi = xi_s[...]
    y_ref[...] = _dot(hr.astype(BF), cre_ref[...]) - _dot(hi.astype(BF), cim_ref[...]) + d_ref[...] * u

    @pl.when(c == t_final // tc)
    def _():
        r = t_final % tc
        hre_ref[...] = jnp.broadcast_to(xr_s[r:r + 1, :], hre_ref.shape)
        him_ref[...] = jnp.broadcast_to(xi_s[r:r + 1, :], him_ref.shape)


def _s5_prompt(proj, bre, bim, cre, cim, d_row, ab_re, ab_im):
    tc = 384
    nt = T_PAD // tc
    body = functools.partial(_s5_prompt_body, tc=tc, t_final=T_REAL - 1)
    wspec_in = pl.BlockSpec((None, LANES, S5_BLK), lambda b, s, c: (s, 0, 0))
    wspec_out = pl.BlockSpec((None, S5_BLK, LANES), lambda b, s, c: (s, 0, 0))
    aspec = pl.BlockSpec((1, S5_BLK), lambda b, s, c: (0, s))
    st_spec = pl.BlockSpec((None, None, SUBLANES, S5_BLK), lambda b, s, c: (b, s, 0, 0))
    st_shape = jax.ShapeDtypeStruct((BATCH, N_S5_BLK, SUBLANES, S5_BLK), F32)
    y, hre, him = pl.pallas_call(
        body,
        grid=(BATCH, N_S5_BLK, nt),
        in_specs=[pl.BlockSpec((tc, LANES), lambda b, s, c: (b * nt + c, s)),
                  wspec_in, wspec_in, wspec_out, wspec_out,
                  pl.BlockSpec((1, LANES), lambda b, s, c: (0, s)),
                  aspec, aspec],
        out_specs=[pl.BlockSpec((tc, LANES), lambda b, s, c: (b * nt + c, s)), st_spec, st_spec],
        out_shape=[jax.ShapeDtypeStruct((ROWS_P, S5_WIDTH), F32), st_shape, st_shape],
        scratch_shapes=[pltpu.VMEM((tc, S5_BLK), F32), pltpu.VMEM((tc, S5_BLK), F32),
                        pltpu.VMEM((1, S5_BLK), F32), pltpu.VMEM((1, S5_BLK), F32)],
        compiler_params=_cp(("parallel", "parallel", "arbitrary")),
        name="s5_prompt",
    )(proj, bre, bim, cre, cim, d_row, ab_re, ab_im)
    fin = lambda h: h[:, :, 0, :].reshape(BATCH, S5_GROUPS, S5_STATE)
    return y, fin(hre), fin(him)


def _s5_sample_body(u_ref, h0r_ref, h0i_ref, bre_ref, bim_ref, cre_ref, cim_ref, d_ref, ar_ref, ai_ref,
                    y_ref, hre_ref, him_ref):
    u = u_ref[...]
    ub = u.astype(BF)
    bur = _dot(ub, bre_ref[...])
    bui = _dot(ub, bim_ref[...])
    ar = ar_ref[...]
    ai = ai_ref[...]
    hr = h0r_ref[...]
    hi = h0i_ref[...]
    hrs, his = [], []
    for t in range(DEC_SEQ):
        lo = t * DEC_BATCH
        hr, hi = (ar * hr - ai * hi + bur[lo:lo + DEC_BATCH], ar * hi + ai * hr + bui[lo:lo + DEC_BATCH])
        hrs.append(hr)
        his.append(hi)
    hra = jnp.concatenate(hrs, axis=0)
    hia = jnp.concatenate(his, axis=0)
    y_ref[...] = _dot(hra.astype(BF), cre_ref[...]) - _dot(hia.astype(BF), cim_ref[...]) + d_ref[...] * u
    hre_ref[...] = hr
    him_ref[...] = hi


def _s5_sample(proj, h0_re, h0_im, bre, bim, cre, cim, d_row, ab_re, ab_im):
    n = S5_GROUPS * S5_STATE
    wspec_in = pl.BlockSpec((None, LANES, S5_BLK), lambda s: (s, 0, 0))
    wspec_out = pl.BlockSpec((None, S5_BLK, LANES), lambda s: (s, 0, 0))
    aspec = pl.BlockSpec((1, S5_BLK), lambda s: (0, s))
    hspec = pl.BlockSpec((DEC_BATCH, S5_BLK), lambda s: (0, s))
    y, hre, him = pl.pallas_call(
        _s5_sample_body,
        grid=(N_S5_BLK,),
        in_specs=[pl.BlockSpec((ROWS_S, LANES), lambda s: (0, s)), hspec, hspec,
                  wspec_in, wspec_in, wspec_out, wspec_out,
                  pl.BlockSpec((1, LANES), lambda s: (0, s)), aspec, aspec],
        out_specs=[pl.BlockSpec((ROWS_S, LANES), lambda s: (0, s)), hspec, hspec],
        out_shape=[jax.ShapeDtypeStruct((ROWS_S, S5_WIDTH), F32),
                   jax.ShapeDtypeStruct((DEC_BATCH, n), F32), jax.ShapeDtypeStruct((DEC_BATCH, n), F32)],
        compiler_params=_cp(("parallel",)),
        name="s5_sample",
    )(proj, h0_re.reshape(DEC_BATCH, n), h0_im.reshape(DEC_BATCH, n), bre, bim, cre, cim, d_row, ab_re, ab_im)
    return y, hre.reshape(DEC_BATCH, S5_GROUPS, S5_STATE), him.reshape(DEC_BATCH, S5_GROUPS, S5_STATE)


def _glu_body(y_ref, w_ref, b_ref, o_ref):
    y = y_ref[...]
    z = 0.5 * y * (1.0 + jnp.tanh(math.sqrt(2.0 / math.pi) * (y + 0.044715 * (y * y * y))))
    o_ref[...] = (z * _sigmoid(_dot(z.astype(BF), w_ref[...]) + b_ref[...])).astype(o_ref.dtype)


def _glu(y, w_b, b_row, layer, tm):
    rows = y.shape[0]
    return pl.pallas_call(
        _glu_body,
        grid=(rows // tm,),
        in_specs=[pl.BlockSpec((tm, S5_WIDTH), lambda i: (i, 0)),
                  pl.BlockSpec((None, S5_WIDTH, S5_WIDTH), lambda i: (layer, 0, 0)),
                  pl.BlockSpec((1, S5_WIDTH), lambda i: (0, 0))],
        out_specs=pl.BlockSpec((tm, S5_WIDTH), lambda i: (i, 0)),
        out_shape=jax.ShapeDtypeStruct((rows, S5_WIDTH), BF),
        compiler_params=_cp(("parallel",)),
        name="s5_glu",
    )(y, w_b, b_row)


def _log_sigmoid(x):
    return jnp.minimum(x, 0.0) - jnp.log1p(jnp.exp(-jnp.abs(x)))


def _split_dot(tri_b, x):
    hi = x.astype(BF)
    r1 = x - hi.astype(F32)
    mid = r1.astype(BF)
    lo = (r1 - mid.astype(F32)).astype(BF)
    return _dot(tri_b, hi) + _dot(tri_b, mid) + _dot(tri_b, lo)


def _mlstm_body(*refs, chunk, n_valid, rows_in, has_init):
    q_ref, k_ref, v_ref, og_ref, g_ref, bias_ref, ng_ref = refs[:7]
    pos = 7
    if has_init:
        c0_ref, n0_ref, m0_ref = refs[pos:pos + 3]
        pos += 3
    y_ref, c_ref, n_ref, m_ref = refs[pos:pos + 4]
    pads = refs[pos + 4:]
    c = pl.program_id(1)
    L = chunk

    @pl.when(c == 0)
    def _():
        if has_init:
            c_ref[...] = c0_ref[...]
            n_ref[...] = n0_ref[...]
            m_ref[...] = m0_ref[...]
        else:
            c_ref[...] = jnp.zeros_like(c_ref)
            n_ref[...] = jnp.zeros_like(n_ref)
            m_ref[...] = jnp.zeros_like(m_ref)

    if rows_in < L:
        @pl.when(pl.program_id(0) == 0)
        def _():
            for p in pads:
                p[...] = jnp.zeros_like(p)

        for src, p in zip((q_ref, k_ref, v_ref, og_ref, g_ref), pads):
            p[0:rows_in, :] = src[...]
        q_src, k_src, v_src, og_src, g_src = pads
    else:
        q_src, k_src, v_src, og_src, g_src = q_ref, k_ref, v_ref, og_ref, g_ref

    rid = lax.broadcasted_iota(jnp.int32, (L, 1), 0) + c * L
    valid = rid < n_valid
    lane = lax.broadcasted_iota(jnp.int32, (1, LANES), 1)
    g = g_src[...] + bias_ref[...]
    lf = jnp.where(valid, _log_sigmoid(g), 0.0)
    x = jnp.where(lane < ML_HEADS, jnp.where(valid, g, NEG), lf)
    ti = lax.broadcasted_iota(jnp.int32, (L, L), 0)
    si = lax.broadcasted_iota(jnp.int32, (L, L), 1)
    causal = si <= ti
    tri_b = jnp.where(causal, 1.0, 0.0).astype(BF)
    bc = _split_dot(tri_b, lf)
    xt = x.T
    bt = bc.T
    scale = ML_HEAD_DIM ** -0.5

    for h in range(ML_HEADS):
        sl = slice(h * ML_HEAD_DIM, (h + 1) * ML_HEAD_DIM)
        q = q_src[:, sl]
        k = k_src[:, sl] * scale
        v = v_src[:, sl]
        ig_col = x[:, h:h + 1]
        ig_row = xt[h:h + 1, :]
        b_col = bc[:, ML_HEADS + h:ML_HEADS + h + 1]
        b_row = bt[ML_HEADS + h:ML_HEADS + h + 1, :]
        m_prev = m_ref[h:h + 1, 0:1]
        cmat = c_ref[h]
        n_row = n_ref[h:h + 1, :]

        log_d = jnp.where(causal, b_col - b_row + ig_row, NEG)
        m_inter = b_col + m_prev
        m_t = jnp.maximum(m_inter, jnp.max(log_d, axis=1, keepdims=True))
        dmat = jnp.exp(log_d - m_t)
        w_inter = jnp.exp(m_inter - m_t)
        qb = q.astype(BF)
        kb = k.astype(BF)
        vb = v.astype(BF)
        s = _dot_nt(qb, kb) * dmat
        num = w_inter * _dot_nt(qb, cmat.astype(BF)) + _dot(s.astype(BF), vb)
        den = w_inter * jnp.sum(q * n_row, axis=1, keepdims=True) + jnp.sum(s, axis=1, keepdims=True)
        hh = num / jnp.maximum(jnp.abs(den), jnp.exp(-m_t))
        mu = jnp.mean(hh, axis=1, keepdims=True)
        hc = hh - mu
        var = jnp.mean(hc * hc, axis=1, keepdims=True)
        hn = hc * lax.rsqrt(var + LN_EPS) * ng_ref[:, sl]
        yv = (_sigmoid(og_src[:, sl]) * hn).astype(y_ref.dtype)
        if rows_in < L:
            y_ref[:, sl] = yv[0:rows_in, :]
        else:
            y_ref[:, sl] = yv

        m_new = m_t[L - 1:L, :]
        w_s = jnp.exp(b_col[L - 1:L, :] - b_col + ig_col - m_new)
        f_s = jnp.exp(m_inter[L - 1:L, :] - m_new)
        vw_t = (v * w_s).T
        c_ref[h] = f_s * cmat + _dot(vw_t.astype(BF), kb)
        n_ref[h:h + 1, :] = f_s * n_row + jnp.sum(k * w_s, axis=0, keepdims=True)
        m_ref[h:h + 1, :] = jnp.broadcast_to(m_new, (1, LANES))


def _mlstm_prompt(proj, gates, bias_row, ng_row):
    L = 128
    nc = T_PAD // L
    body = functools.partial(_mlstm_body, chunk=L, n_valid=T_REAL, rows_in=L, has_init=False)
    col = lambda off: pl.BlockSpec((L, ML_WIDTH), lambda b, c: (b * nc + c, off))
    st = lambda *tail: pl.BlockSpec((None,) + tail, lambda b, c: (b,) + (0,) * len(tail))
    y, cst, nst, mst = pl.pallas_call(
        body,
        grid=(BATCH, nc),
        in_specs=[col(1), col(2), col(3), col(4),
                  pl.BlockSpec((L, LANES), lambda b, c: (b * nc + c, 0)),
                  pl.BlockSpec((1, LANES), lambda b, c: (0, 0)),
                  pl.BlockSpec((1, ML_WIDTH), lambda b, c: (0, 0))],
        out_specs=[pl.BlockSpec((L, ML_WIDTH), lambda b, c: (b * nc + c, 0)),
                   st(ML_HEADS, ML_HEAD_DIM, ML_HEAD_DIM), st(ML_HEADS, ML_HEAD_DIM), st(ML_HEADS, LANES)],
        out_shape=[jax.ShapeDtypeStruct((ROWS_P, ML_WIDTH), BF),
                   jax.ShapeDtypeStruct((BATCH, ML_HEADS, ML_HEAD_DIM, ML_HEAD_DIM), F32),
                   jax.ShapeDtypeStruct((BATCH, ML_HEADS, ML_HEAD_DIM), F32),
                   jax.ShapeDtypeStruct((BATCH, ML_HEADS, LANES), F32)],
        compiler_params=_cp(("parallel", "arbitrary")),
        name="mlstm_prompt",
    )(proj, proj, proj, proj, gates, bias_row, ng_row)
    return y, cst, nst, mst[:, :, 0]


def _mlstm_sample(proj_bm, gates_bm, bias_row, ng_row, c0, n0, m0, layer):
    L = 128
    body = functools.partial(_mlstm_body, chunk=L, n_valid=DEC_SEQ, rows_in=DEC_SEQ, has_init=True)
    col = lambda off: pl.BlockSpec((None, DEC_SEQ, ML_WIDTH), lambda b, c: (b, 0, off))
    st_in = lambda *tail: pl.BlockSpec((None, None) + tail, lambda b, c: (layer, b) + (0,) * len(tail))
    st = lambda *tail: pl.BlockSpec((None,) + tail, lambda b, c: (b,) + (0,) * len(tail))
    m0b = jnp.broadcast_to(m0[..., None], m0.shape + (LANES,))
    y, cst, nst, mst = pl.pallas_call(
        body,
        grid=(DEC_BATCH, 1),
        in_specs=[col(1), col(2), col(3), col(4),
                  pl.BlockSpec((None, DEC_SEQ, LANES), lambda b, c: (b, 0, 0)),
                  pl.BlockSpec((1, LANES), lambda b, c: (0, 0)),
                  pl.BlockSpec((1, ML_WIDTH), lambda b, c: (0, 0)),
                  st_in(ML_HEADS, ML_HEAD_DIM, ML_HEAD_DIM), st_in(ML_HEADS, ML_HEAD_DIM), st_in(ML_HEADS, LANES)],
        out_specs=[pl.BlockSpec((None, DEC_SEQ, ML_WIDTH), lambda b, c: (b, 0, 0)),
                   st(ML_HEADS, ML_HEAD_DIM, ML_HEAD_DIM), st(ML_HEADS, ML_HEAD_DIM), st(ML_HEADS, LANES)],
        out_shape=[jax.ShapeDtypeStruct((DEC_BATCH, DEC_SEQ, ML_WIDTH), BF),
                   jax.ShapeDtypeStruct((DEC_BATCH, ML_HEADS, ML_HEAD_DIM, ML_HEAD_DIM), F32),
                   jax.ShapeDtypeStruct((DEC_BATCH, ML_HEADS, ML_HEAD_DIM), F32),
                   jax.ShapeDtypeStruct((DEC_BATCH, ML_HEADS, LANES), F32)],
        scratch_shapes=[pltpu.VMEM((L, ML_WIDTH), F32)] * 4 + [pltpu.VMEM((L, LANES), F32)],
        compiler_params=_cp(("arbitrary", "arbitrary")),
        name="mlstm_sample",
    )(proj_bm, proj_bm, proj_bm, proj_bm, gates_bm, bias_row, ng_row, c0, n0, m0b)
    return y, cst, nst, mst[:, :, 0]


def _proj_ln_body(*refs, n_in):
    a_refs = refs[:n_in]
    w_refs = refs[n_in:2 * n_in]
    x_ref, g_ref, b_ref, o_ref = refs[2 * n_in:]
    y = _dot(a_refs[0][...].astype(BF), w_refs[0][...])
    for a_ref, w_ref in zip(a_refs[1:], w_refs[1:]):
        y = y + _dot(a_ref[...].astype(BF), w_ref[...])
    o_ref[...] = _layer_norm(ALPHA * x_ref[...] + y, g_ref[...], b_ref[...])


def _proj_ln(acts, w_b, layer, x, g_row, b_row, tm):
    rows = x.shape[0]
    n_in = len(acts)
    a_specs, w_specs = [], []
    for a in acts:
        kk = a.shape[1]
        slab = len(w_specs)
        a_specs.append(pl.BlockSpec((tm, kk), lambda i: (i, 0)))
        w_specs.append(pl.BlockSpec((None, kk, D_MODEL), lambda i, slab=slab: (layer, slab, 0)))
    row_spec = pl.BlockSpec((1, D_MODEL), lambda i: (0, 0))
    return pl.pallas_call(
        functools.partial(_proj_ln_body, n_in=n_in),
        grid=(rows // tm,),
        in_specs=a_specs + w_specs + [pl.BlockSpec((tm, D_MODEL), lambda i: (i, 0)), row_spec, row_spec],
        out_specs=pl.BlockSpec((tm, D_MODEL), lambda i: (i, 0)),
        out_shape=jax.ShapeDtypeStruct((rows, D_MODEL), F32),
        compiler_params=_cp(("parallel",)),
        name="proj_ln",
    )(*acts, *([w_b] * n_in), x, g_row, b_row)


HALO = BF16_ROWS


def _conv3(up, cw_ref, cb_ref, lo, n):
    cw = cw_ref[...]
    r1 = pltpu.roll(up, 1, 0)
    r2 = pltpu.roll(up, 2, 0)
    return (cb_ref[...] + cw[2:3, :] * up[lo:lo + n] + cw[1:2, :] * r1[lo:lo + n]
            + cw[0:1, :] * r2[lo:lo + n])


def _ffn_prompt_body(x_ref, xh_ref, wv_ref, wg_ref, cwv_ref, cwg_ref, cbv_ref, cbg_ref, wd_ref,
                     g_ref, b_ref, o_ref, xb_s, acc_s, *, tm, tiles_per_seq):
    i = pl.program_id(0)
    j = pl.program_id(1)

    @pl.when(j == 0)
    def _():
        first = (i % tiles_per_seq) == 0
        halo = jnp.where(first, 0.0, xh_ref[...])
        xb_s[0:HALO, :] = halo.astype(BF)
        xb_s[HALO:, :] = x_ref[...].astype(BF)
        acc_s[...] = jnp.zeros_like(acc_s)

    xb = xb_s[...]
    hv = _conv3(_dot(xb, wv_ref[...]), cwv_ref, cbv_ref, HALO, tm)
    hg = _conv3(_dot(xb, wg_ref[...]), cwg_ref, cbg_ref, HALO, tm)
    act = hg * _sigmoid(hg) * hv
    acc_s[...] += _dot(act.astype(BF), wd_ref[...])

    @pl.when(j == pl.num_programs(1) - 1)
    def _():
        o_ref[...] = _layer_norm(ALPHA * x_ref[...] + acc_s[...], g_ref[...], b_ref[...])


def _ffn_prompt(x, w_up_b, conv_w, conv_b, w_down_b, layer, g_row, b_row):
    tm, tf = 704, 512
    nf = D_FF // tf
    tiles_per_seq = T_PAD // tm
    body = functools.partial(_ffn_prompt_body, tm=tm, tiles_per_seq=tiles_per_seq)
    halo_blocks = tm // HALO
    row_spec = pl.BlockSpec((1, D_MODEL), lambda i, j: (0, 0))
    return pl.pallas_call(
        body,
        grid=(ROWS_P // tm, nf),
        in_specs=[pl.BlockSpec((tm, D_MODEL), lambda i, j: (i, 0)),
                  pl.BlockSpec((HALO, D_MODEL), lambda i, j: (jnp.maximum(i * halo_blocks - 1, 0), 0)),
                  pl.BlockSpec((None, D_MODEL, tf), lambda i, j: (layer, 0, j)),
                  pl.BlockSpec((None, D_MODEL, tf), lambda i, j: (layer, 0, j + nf)),
                  pl.BlockSpec((None, 3, tf), lambda i, j: (layer, 0, j)),
                  pl.BlockSpec((None, 3, tf), lambda i, j: (layer, 0, j + nf)),
                  pl.BlockSpec((None, 1, tf), lambda i, j: (layer, 0, j)),
                  pl.BlockSpec((None, 1, tf), lambda i, j: (layer, 0, j + nf)),
                  pl.BlockSpec((None, tf, D_MODEL), lambda i, j: (layer, j, 0)),
                  row_spec, row_spec],
        out_specs=pl.BlockSpec((tm, D_MODEL), lambda i, j: (i, 0)),
        out_shape=jax.ShapeDtypeStruct((ROWS_P, D_MODEL), F32),
        scratch_shapes=[pltpu.VMEM((tm + HALO, D_MODEL), BF), pltpu.VMEM((tm, D_MODEL), F32)],
        compiler_params=_cp(("parallel", "arbitrary")),
        name="ffn_prompt",
    )(x, x, w_up_b, w_up_b, conv_w, conv_w, conv_b, conv_b, w_down_b, g_row, b_row)


EXTRA = SUBLANES


def _ffn_sample_body(x_ref, p0v_ref, p0g_ref, p1v_ref, p1g_ref, wv_ref, wg_ref, cwv_ref, cwg_ref,
                     cbv_ref, cbg_ref, wd_ref, g_ref, b_ref, o_ref, cs_ref, cp_ref, xb_s, acc_s):
    j = pl.program_id(0)

    @pl.when(j == 0)
    def _():
        xb_s[...] = x_ref[...].astype(BF)
        acc_s[...] = jnp.zeros_like(acc_s)

    xb = xb_s[...]
    n = ROWS_S
    bsz = DEC_BATCH

    def conv(up, p0_ref, p1_ref, cw_ref, cb_ref):
        cw = cw_ref[...]
        ext = jnp.concatenate([p0_ref[...], p1_ref[...], up[0:n]], axis=0)
        return (cb_ref[...] + cw[0:1, :] * ext[0:n] + cw[1:2, :] * ext[bsz:bsz + n]
                + cw[2:3, :] * ext[2 * bsz:2 * bsz + n])

    upv = _dot(xb, wv_ref[...])
    upg = _dot(xb, wg_ref[...])
    hv = conv(upv, p0v_ref, p1v_ref, cwv_ref, cbv_ref)
    hg = conv(upg, p0g_ref, p1g_ref, cwg_ref, cbg_ref)
    act = hg * _sigmoid(hg) * hv
    acc_s[...] += _dot(act.astype(BF), wd_ref[...])
    lo = (DEC_SEQ - 2) * bsz
    cs_ref[0] = upv[lo:lo + bsz]
    cs_ref[1] = upg[lo:lo + bsz]
    cs_ref[2] = upv[lo + bsz:lo + 2 * bsz]
    cs_ref[3] = upg[lo + bsz:lo + 2 * bsz]
    cp_ref[0] = upv[n:n + EXTRA]
    cp_ref[1] = upg[n:n + EXTRA]

    @pl.when(j == pl.num_programs(0) - 1)
    def _():
        o_ref[...] = _layer_norm(ALPHA * x_ref[0:n, :] + acc_s[...], g_ref[...], b_ref[...])


def _ffn_sample(x_ext, prev, w_up_b, conv_w, conv_b, w_down_b, layer, g_row, b_row):
    tf = 512
    nf = D_FF // tf
    rows = ROWS_S + EXTRA
    prev_spec = lambda q: pl.BlockSpec((None, DEC_BATCH, tf), lambda j: (layer, 0, j + q * nf))
    row_spec = pl.BlockSpec((1, D_MODEL), lambda j: (0, 0))
    return pl.pallas_call(
        _ffn_sample_body,
        grid=(nf,),
        in_specs=[pl.BlockSpec((rows, D_MODEL), lambda j: (0, 0)),
                  prev_spec(0), prev_spec(1), prev_spec(2), prev_spec(3),
                  pl.BlockSpec((None, D_MODEL, tf), lambda j: (layer, 0, j)),
                  pl.BlockSpec((None, D_MODEL, tf), lambda j: (layer, 0, j + nf)),
                  pl.BlockSpec((None, 3, tf), lambda j: (layer, 0, j)),
                  pl.BlockSpec((None, 3, tf), lambda j: (layer, 0, j + nf)),
                  pl.BlockSpec((None, 1, tf), lambda j: (layer, 0, j)),
                  pl.BlockSpec((None, 1, tf), lambda j: (layer, 0, j + nf)),
                  pl.BlockSpec((None, tf, D_MODEL), lambda j: (layer, j, 0)),
                  row_spec, row_spec],
        out_specs=[pl.BlockSpec((ROWS_S, D_MODEL), lambda j: (0, 0)),
                   pl.BlockSpec((4, DEC_BATCH, tf), lambda j: (0, 0, j)),
                   pl.BlockSpec((2, EXTRA, tf), lambda j: (0, 0, j))],
        out_shape=[jax.ShapeDtypeStruct((ROWS_S, D_MODEL), F32),
                   jax.ShapeDtypeStruct((4, DEC_BATCH, D_FF), F32),
                   jax.ShapeDtypeStruct((2, EXTRA, D_FF), F32)],
        scratch_shapes=[pltpu.VMEM((rows, D_MODEL), BF), pltpu.VMEM((ROWS_S, D_MODEL), F32)],
        compiler_params=_cp(("arbitrary",)),
        name="ffn_sample",
    )(x_ext, prev, prev, prev, prev, w_up_b, w_up_b, conv_w, conv_w, conv_b, conv_b, w_down_b, g_row, b_row)


def _rms(xf, g):
    return xf * lax.rsqrt(jnp.mean(xf * xf, axis=-1, keepdims=True) + RMS_EPS) * g


def _mla_in_body(x_ref, win_ref, gq_ref, gkv_ref, wuq_ref, cos_ref, sin_ref,
                 ckv_ref, kk_ref, qn_ref, qp_ref):
    p = _dot(x_ref[...].astype(BF), win_ref[...])
    cq = _rms(p[:, 0:Q_LORA], gq_ref[...])
    ckv_ref[...] = _rms(p[:, Q_LORA:Q_LORA + KV_LORA], gkv_ref[...])
    cos = cos_ref[...]
    sin = sin_ref[...]
    o = Q_LORA + KV_LORA
    kk_ref[...] = p[:, o:o + LANES] * cos + p[:, o + LANES:o + 2 * LANES] * sin
    q = _dot(cq.astype(BF), wuq_ref[...])
    n_nope = MLA_HEADS * QK_NOPE
    n_pe = MLA_HEADS * QK_ROPE
    qn_ref[...] = q[:, 0:n_nope].astype(BF)
    reps = n_pe // LANES
    cos_w = jnp.concatenate([cos] * reps, axis=1)
    sin_w = jnp.concatenate([sin] * reps, axis=1)
    qp_ref[...] = (q[:, n_nope:n_nope + n_pe] * cos_w + q[:, n_nope + n_pe:] * sin_w).astype(BF)


def _mla_in(x, win_b, gq_row, gkv_row, wuq_b, cos_t, sin_t, layer, tm):
    rows = x.shape[0]
    n_in = win_b.shape[-1]
    n_q = wuq_b.shape[-1]
    row = lambda w: pl.BlockSpec((tm, w), lambda i: (i, 0))
    return pl.pallas_call(
        _mla_in_body,
        grid=(rows // tm,),
        in_specs=[row(D_MODEL),
                  pl.BlockSpec((None, D_MODEL, n_in), lambda i: (layer, 0, 0)),
                  pl.BlockSpec((1, Q_LORA), lambda i: (0, 0)),
                  pl.BlockSpec((1, KV_LORA), lambda i: (0, 0)),
                  pl.BlockSpec((None, Q_LORA, n_q), lambda i: (layer, 0, 0)),
                  row(LANES), row(LANES)],
        out_specs=[row(KV_LORA), row(LANES), row(MLA_HEADS * QK_NOPE), row(MLA_HEADS * QK_ROPE)],
        out_shape=[jax.ShapeDtypeStruct((rows, KV_LORA), F32),
                   jax.ShapeDtypeStruct((rows, LANES), F32),
                   jax.ShapeDtypeStruct((rows, MLA_HEADS * QK_NOPE), BF),
                   jax.ShapeDtypeStruct((rows, MLA_HEADS * QK_ROPE), BF)],
        compiler_params=_cp(("parallel",)),
        name="mla_in",
    )(x, win_b, gq_row, gkv_row, wuq_b, cos_t, sin_t)


def _flash_body(qn_ref, qp_ref, kn_ref, kk_ref, v_ref, o_ref, m_s, l_s, acc_s, *, tq):
    qi = pl.program_id(2)
    lane = lax.broadcasted_iota(jnp.int32, (1, LANES), 1)
    qp = qp_ref[...].astype(F32)
    qn = qn_ref[...]
    half = LANES // 2
    qs = (jnp.concatenate([qn[:, 0:QK_NOPE], jnp.where(lane < half, qp, 0.0).astype(BF)], axis=1),
          jnp.concatenate([qn[:, QK_NOPE:], jnp.where(lane >= half, qp, 0.0).astype(BF)], axis=1))
    m_s[...] = jnp.full_like(m_s, NEG)
    l_s[...] = jnp.zeros_like(l_s)
    acc_s[...] = jnp.zeros_like(acc_s)
    ti = lax.broadcasted_iota(jnp.int32, (tq, tq), 0)
    si = lax.broadcasted_iota(jnp.int32, (tq, tq), 1)
    causal = si <= ti

    def chunk(kc, masked):
        off = pl.multiple_of(kc * tq, tq)
        kn = kn_ref[pl.ds(off, tq), :]
        kk = kk_ref[pl.ds(off, tq), :].astype(BF)
        vv = v_ref[pl.ds(off, tq), :]
        for hh in range(2):
            kh = jnp.concatenate([kn[:, hh * QK_NOPE:(hh + 1) * QK_NOPE], kk], axis=1)
            s = _dot_nt(qs[hh], kh) * MLA_SCALE
            if masked:
                s = jnp.where(causal, s, NEG)
            m_prev = m_s[hh]
            m_new = jnp.maximum(m_prev, jnp.max(s, axis=1, keepdims=True))
            alpha = jnp.exp(m_prev - m_new)
            p = jnp.exp(s - m_new)
            l_s[hh] = alpha * l_s[hh] + jnp.sum(p, axis=1, keepdims=True)
            acc_s[hh] = alpha * acc_s[hh] + _dot(p.astype(BF), vv[:, hh * V_DIM:(hh + 1) * V_DIM])
            m_s[hh] = m_new

    def loop_body(kc, carry):
        chunk(kc, False)
        return carry

    lax.fori_loop(0, qi, loop_body, 0)
    chunk(qi, True)
    o_ref[...] = jnp.concatenate([acc_s[0] / l_s[0], acc_s[1] / l_s[1]], axis=1).astype(o_ref.dtype)


def _flash(qn, qp, kn, kk, v):
    tq = 384
    nq = T_PAD // tq
    body = functools.partial(_flash_body, tq=tq)
    two = 2 * QK_NOPE
    return pl.pallas_call(
        body,
        grid=(BATCH, MLA_HEADS // 2, nq),
        in_specs=[pl.BlockSpec((tq, two), lambda b, h, q: (b * nq + q, h)),
                  pl.BlockSpec((tq, LANES), lambda b, h, q: (b * nq + q, h)),
                  pl.BlockSpec((T_PAD, two), lambda b, h, q: (b, h)),
                  pl.BlockSpec((T_PAD, LANES), lambda b, h, q: (b, 0)),
                  pl.BlockSpec((T_PAD, two), lambda b, h, q: (b, h))],
        out_specs=pl.BlockSpec((tq, two), lambda b, h, q: (b * nq + q, h)),
        out_shape=jax.ShapeDtypeStruct((ROWS_P, MLA_HEADS * V_DIM), BF),
        scratch_shapes=[pltpu.VMEM((2, tq, 1), F32), pltpu.VMEM((2, tq, 1), F32),
                        pltpu.VMEM((2, tq, V_DIM), F32)],
        compiler_params=_cp(("parallel", "parallel", "arbitrary")),
        name="mla_flash",
    )(qn, qp, kn, kk, v)


def _absorb_body(qn_ref, wuk_ref, o_ref):
    o_ref[...] = _dot_nt(qn_ref[...], wuk_ref[...]).astype(o_ref.dtype)


def _absorb(qn, wuk_b, layer):
    return pl.pallas_call(
        _absorb_body,
        grid=(MLA_HEADS,),
        in_specs=[pl.BlockSpec((ROWS_S, QK_NOPE), lambda h: (0, h)),
                  pl.BlockSpec((None, KV_LORA, QK_NOPE), lambda h: (layer, 0, h))],
        out_specs=pl.BlockSpec((ROWS_S, KV_LORA), lambda h: (0, h)),
        out_shape=jax.ShapeDtypeStruct((ROWS_S, MLA_HEADS * KV_LORA), BF),
        compiler_params=_cp(("parallel",)),
        name="mla_absorb",
    )(qn, wuk_b)


def _unabsorb_body(ol_ref, wuv_ref, o_ref):
    o_ref[...] = _dot(ol_ref[...], wuv_ref[...]).astype(o_ref.dtype)


def _unabsorb(ol, wuv_b, layer):
    return pl.pallas_call(
        _unabsorb_body,
        grid=(MLA_HEADS,),
        in_specs=[pl.BlockSpec((ROWS_S, KV_LORA), lambda h: (0, h)),
                  pl.BlockSpec((None, KV_LORA, V_DIM), lambda h: (layer, 0, h))],
        out_specs=pl.BlockSpec((ROWS_S, V_DIM), lambda h: (0, h)),
        out_shape=jax.ShapeDtypeStruct((ROWS_S, MLA_HEADS * V_DIM), BF),
        compiler_params=_cp(("parallel",)),
        name="mla_unabsorb",
    )(ol, wuv_b)


QROWS = DEC_SEQ * MLA_HEADS


def _paged_body(pt_ref, ql_ref, qp_ref, cn_ref, kn_ref, *rest, n_pages):
    ckv_refs = rest[:n_pages]
    kpe_refs = rest[n_pages:2 * n_pages]
    o_ref, m_s, l_s, acc_s, newc_s, newk_s = rest[2 * n_pages:]
    j = pl.program_id(1)

    @pl.when(j == 0)
    def _():
        m_s[...] = jnp.full_like(m_s, NEG)
        l_s[...] = jnp.zeros_like(l_s)
        acc_s[...] = jnp.zeros_like(acc_s)

    ql = ql_ref[...].reshape(QROWS, KV_LORA)
    qp = qp_ref[...].reshape(QROWS, QK_ROPE)

    def update(s, vb):
        m_prev = m_s[...]
        m_new = jnp.maximum(m_prev, jnp.max(s, axis=1, keepdims=True))
        alpha = jnp.exp(m_prev - m_new)
        p = jnp.exp(s - m_new)
        l_s[...] = alpha * l_s[...] + jnp.sum(p, axis=1, keepdims=True)
        acc_s[...] = alpha * acc_s[...] + _dot(p.astype(BF), vb)
        m_s[...] = m_new

    cb = jnp.concatenate([r[...].astype(BF) for r in ckv_refs], axis=0)
    kb = jnp.concatenate([r[...].astype(BF) for r in kpe_refs], axis=0)
    update((_dot_nt(ql, cb) + _dot_nt(qp, kb)) * MLA_SCALE, cb)

    @pl.when(j == pl.num_programs(1) - 1)
    def _():
        newc_s[...] = jnp.zeros_like(newc_s)
        newk_s[...] = jnp.zeros_like(newk_s)
        newc_s[0:DEC_SEQ, :] = cn_ref[...]
        newk_s[0:DEC_SEQ, :] = kn_ref[...]
        cn = newc_s[...].astype(BF)
        s = (_dot_nt(ql, cn) + _dot_nt(qp, newk_s[...].astype(BF))) * MLA_SCALE
        qt = lax.broadcasted_iota(jnp.int32, (QROWS, LANES), 0) // MLA_HEADS
        kt = lax.broadcasted_iota(jnp.int32, (QROWS, LANES), 1)
        update(jnp.where(kt <= qt, s, NEG), cn)
        o = acc_s[...] / l_s[...]
        o_ref[...] = o.reshape(DEC_SEQ, MLA_HEADS, KV_LORA).astype(o_ref.dtype)


def _paged_attention(page_table, q_lat, q_pe, ckv_new, kpe_new, cache_ckv, cache_kpe, layer):
    n_pages = PAGES_PER_STEP
    n_steps = page_table.shape[1] // n_pages
    page = cache_ckv.shape[2]
    body = functools.partial(_paged_body, n_pages=n_pages)
    qspec = lambda w: pl.BlockSpec((DEC_SEQ, None, MLA_HEADS, w), lambda b, j, pt: (0, b, 0, 0))
    nspec = lambda w: pl.BlockSpec((None, DEC_SEQ, w), lambda b, j, pt: (b, 0, 0))

    def pspec(w, i):
        return pl.BlockSpec((None, None, page, w),
                            lambda b, j, pt, i=i: (layer, pt[b, j * n_pages + i], 0, 0))

    grid_spec = pltpu.PrefetchScalarGridSpec(
        num_scalar_prefetch=1,
        grid=(DEC_BATCH, n_steps),
        in_specs=[qspec(KV_LORA), qspec(QK_ROPE), nspec(KV_LORA), nspec(QK_ROPE)]
                 + [pspec(KV_LORA, i) for i in range(n_pages)]
                 + [pspec(QK_ROPE, i) for i in range(n_pages)],
        out_specs=pl.BlockSpec((DEC_SEQ, None, MLA_HEADS, KV_LORA), lambda b, j, pt: (0, b, 0, 0)),
        scratch_shapes=[pltpu.VMEM((QROWS, 1), F32), pltpu.VMEM((QROWS, 1), F32),
                        pltpu.VMEM((QROWS, KV_LORA), F32),
                        pltpu.VMEM((LANES, KV_LORA), F32), pltpu.VMEM((LANES, QK_ROPE), F32)],
    )
    return pl.pallas_call(
        body,
        grid_spec=grid_spec,
        out_shape=jax.ShapeDtypeStruct((DEC_SEQ, DEC_BATCH, MLA_HEADS, KV_LORA), BF),
        compiler_params=_cp(("parallel", "arbitrary")),
        name="mla_paged",
    )(page_table, q_lat, q_pe, ckv_new, kpe_new, *([cache_ckv] * n_pages), *([cache_kpe] * n_pages))


def _rope_tables(pos):
    half = QK_ROPE // 2
    inv = ROPE_THETA ** (-jnp.arange(half, dtype=F32) / half)
    ang = pos.astype(F32)[:, None] * inv[None, :]
    cos, sin = jnp.cos(ang), jnp.sin(ang)
    return (jnp.concatenate([cos, cos, cos, cos], axis=1),
            jnp.concatenate([-sin, sin, -sin, sin], axis=1))


def _swap_halves(w):
    half = w.shape[-1] // 2
    return jnp.concatenate([w[..., half:], w[..., :half]], axis=-1)


def kernel(x_prompt, x_sample, cache_mla_ckv, cache_mla_kpe, page_table, state_s5_re, state_s5_im, state_mlstm_c, state_mlstm_n, state_mlstm_m, state_ffn_conv, meta_tokens, ln1_g, ln1_b, ln2_g, ln2_b, mix_w_in, mix_b_gates, s5_a_re, s5_a_im, s5_log_dt, s5_b_re, s5_b_im, s5_c_re, s5_c_im, s5_d, s5_w_glu, s5_b_glu, ml_norm_g, mix_w_out, mla_w_in, mla_q_norm_g, mla_kv_norm_g, mla_w_uq, mla_w_uk, mla_w_uv, mla_w_out, ffn_w_up, ffn_conv_w, ffn_conv_b, ffn_w_down):
    n_mix = mix_w_in.shape[0]
    n_mla = mla_w_in.shape[0]
    past = page_table.shape[1] * cache_mla_ckv.shape[2]

    n_main = S5_WIDTH + 4 * ML_WIDTH
    mix_w_in_b = mix_w_in.astype(BF)
    mix_w_g_b = jnp.pad(mix_w_in[:, :, n_main:], ((0, 0), (0, 0), (0, LANES - 2 * ML_HEADS))).astype(BF)
    gate_bias = jnp.pad(mix_b_gates, ((0, 0), (0, LANES - 2 * ML_HEADS)))
    w_glu_b = s5_w_glu.astype(BF)
    mix_w_out_b = mix_w_out.astype(BF)
    o = Q_LORA + KV_LORA
    w_kpe = mla_w_in[:, :, o:]
    w_kpe_sw = _swap_halves(w_kpe)
    mla_w_in_b = jnp.concatenate([mla_w_in[:, :, :o], w_kpe, w_kpe, w_kpe_sw, w_kpe_sw], axis=-1).astype(BF)
    wq4 = mla_w_uq.reshape(n_mla, Q_LORA, MLA_HEADS, QK_NOPE + QK_ROPE)
    wq_pe = wq4[..., QK_NOPE:]
    mla_w_uq_b = jnp.concatenate(
        [wq4[..., :QK_NOPE].reshape(n_mla, Q_LORA, -1), wq_pe.reshape(n_mla, Q_LORA, -1),
         _swap_halves(wq_pe).reshape(n_mla, Q_LORA, -1)], axis=-1).astype(BF)
    mla_w_uk_b = mla_w_uk.reshape(n_mla, KV_LORA, MLA_HEADS * QK_NOPE).astype(BF)
    mla_w_uv_b = mla_w_uv.reshape(n_mla, KV_LORA, MLA_HEADS * V_DIM).astype(BF)
    mla_w_out_b = mla_w_out.astype(BF)
    ffn_w_up_b = ffn_w_up.astype(BF)
    ffn_w_down_b = ffn_w_down.astype(BF)
    conv_b3 = ffn_conv_b[:, None, :]
    prev_conv = state_ffn_conv.reshape(DEPTH, DEC_BATCH, 4 * D_FF)
    cos_p, sin_p = _rope_tables(jnp.arange(T_PAD))
    cos_p = jnp.concatenate([cos_p] * BATCH, axis=0)
    sin_p = jnp.concatenate([sin_p] * BATCH, axis=0)
    cos_s, sin_s = _rope_tables(past + jnp.arange(DEC_SEQ))
    cos_s = jnp.repeat(cos_s, DEC_BATCH, axis=0)
    sin_s = jnp.repeat(sin_s, DEC_BATCH, axis=0)

    xp = jnp.concatenate([jnp.broadcast_to(meta_tokens[None], (BATCH, N_META, D_MODEL)), x_prompt], axis=1)
    xp = jnp.pad(xp, ((0, 0), (0, T_PAD - T_REAL), (0, 0))).reshape(ROWS_P, D_MODEL)
    xs = x_sample.transpose(1, 0, 2).reshape(ROWS_S, D_MODEL)
    tm_p, tm_s = 704, ROWS_S

    outs = {k: [] for k in ("ckv_p", "kpe_p", "ckv_s", "kpe_s", "s5r_p", "s5i_p", "s5r_s", "s5i_s",
                            "mc_p", "mn_p", "mm_p", "mc_s", "mn_s", "mm_s", "conv_p", "conv_s")}

    for l in range(DEPTH):
        j = l // 2
        g1, b1 = ln1_g[l][None, :], ln1_b[l][None, :]
        g2, b2 = ln2_g[l][None, :], ln2_b[l][None, :]
        if l % 2 == 0:
            ab_re, ab_im, bb_re, bb_im = _s5_params(s5_a_re[j], s5_a_im[j], s5_log_dt[j], s5_b_re[j], s5_b_im[j])
            bre, bim = _blockdiag_in(bb_re), _blockdiag_in(bb_im)
            cre, cim = _blockdiag_out(s5_c_re[j]), _blockdiag_out(s5_c_im[j])
            d_row = s5_d[j].reshape(1, S5_WIDTH)
            bias_row = gate_bias[j][None, :]
            ng_row = ml_norm_g[j][None, :]
            bglu_row = s5_b_glu[j][None, :]

            proj, gates = _mix_proj(xp, mix_w_in_b, mix_w_g_b, j, tm_p)
            y5, hr, hi = _s5_prompt(proj, bre, bim, cre, cim, d_row, ab_re, ab_im)
            y5 = _glu(y5, w_glu_b, bglu_row, j, tm_p)
            yml, c1, n1, m1 = _mlstm_prompt(proj, gates, bias_row, ng_row)
            xp = _proj_ln([y5, yml], mix_w_out_b, j, xp, g1, b1, 352)
            outs["s5r_p"].append(hr); outs["s5i_p"].append(hi)
            outs["mc_p"].append(c1); outs["mn_p"].append(n1); outs["mm_p"].append(m1)

            proj, gates = _mix_proj(xs, mix_w_in_b, mix_w_g_b, j, tm_s)
            y5, hr, hi = _s5_sample(proj, state_s5_re[j], state_s5_im[j], bre, bim, cre, cim, d_row, ab_re, ab_im)
            y5 = _glu(y5, w_glu_b, bglu_row, j, tm_s)
            proj_bm = proj.reshape(DEC_SEQ, DEC_BATCH, n_main).transpose(1, 0, 2)
            gates_bm = gates.reshape(DEC_SEQ, DEC_BATCH, LANES).transpose(1, 0, 2)
            yml, c1, n1, m1 = _mlstm_sample(proj_bm, gates_bm, bias_row, ng_row,
                                            state_mlstm_c, state_mlstm_n, state_mlstm_m, j)
            yml = yml.transpose(1, 0, 2).reshape(ROWS_S, ML_WIDTH)
            xs = _proj_ln([y5, yml], mix_w_out_b, j, xs, g1, b1, 256)
            outs["s5r_s"].append(hr); outs["s5i_s"].append(hi)
            outs["mc_s"].append(c1); outs["mn_s"].append(n1); outs["mm_s"].append(m1)
        else:
            gq_row = mla_q_norm_g[j][None, :]
            gkv_row = mla_kv_norm_g[j][None, :]

            ckv, kk, qn, qp = _mla_in(xp, mla_w_in_b, gq_row, gkv_row, mla_w_uq_b, cos_p, sin_p, j, 352)
            kn = _mm(ckv, mla_w_uk_b, j, BF, 384, MLA_HEADS * QK_NOPE)
            vv = _mm(ckv, mla_w_uv_b, j, BF, 384, MLA_HEADS * V_DIM)
            att = _flash(qn, qp, kn, kk, vv)
            xp = _proj_ln([att], mla_w_out_b, j, xp, g1, b1, 352)
            outs["ckv_p"].append(ckv.reshape(BATCH, T_PAD, KV_LORA)[:, :T_REAL])
            outs["kpe_p"].append(kk.reshape(BATCH, T_PAD, LANES)[:, :T_REAL, :QK_ROPE])

            ckv, kk, qn, qp = _mla_in(xs, mla_w_in_b, gq_row, gkv_row, mla_w_uq_b, cos_s, sin_s, j, 256)
            q_lat = _absorb(qn, mla_w_uk_b, j)
            ckv_bm = ckv.reshape(DEC_SEQ, DEC_BATCH, KV_LORA).transpose(1, 0, 2)
            kpe_bm = kk[:, :QK_ROPE].reshape(DEC_SEQ, DEC_BATCH, QK_ROPE).transpose(1, 0, 2)
            o_lat = _paged_attention(
                page_table,
                q_lat.reshape(DEC_SEQ, DEC_BATCH, MLA_HEADS, KV_LORA),
                qp.reshape(DEC_SEQ, DEC_BATCH, MLA_HEADS, QK_ROPE),
                ckv_bm, kpe_bm, cache_mla_ckv, cache_mla_kpe, j)
            att = _unabsorb(o_lat.reshape(ROWS_S, MLA_HEADS * KV_LORA), mla_w_uv_b, j)
            xs = _proj_ln([att], mla_w_out_b, j, xs, g1, b1, 256)
            outs["ckv_s"].append(ckv_bm)
            outs["kpe_s"].append(kpe_bm)

        tail = jnp.concatenate([xp[b * T_PAD + T_REAL - 2:b * T_PAD + T_REAL] for b in range(BATCH)]
                               + [jnp.zeros((EXTRA - 2 * BATCH, D_MODEL), F32)], axis=0)
        xs_ext = jnp.concatenate([xs, tail], axis=0)
        xs, cs, cpv = _ffn_sample(xs_ext, prev_conv, ffn_w_up_b, ffn_conv_w, conv_b3, ffn_w_down_b, l, g2, b2)
        xp = _ffn_prompt(xp, ffn_w_up_b, ffn_conv_w, conv_b3, ffn_w_down_b, l, g2, b2)
        outs["conv_s"].append(cs.transpose(1, 0, 2).reshape(DEC_BATCH, 2, 2 * D_FF))
        cp2 = jnp.concatenate([cpv[0, :2 * BATCH], cpv[1, :2 * BATCH]], axis=-1)
        outs["conv_p"].append(cp2.reshape(BATCH, 2, 2 * D_FF))

    st = lambda k: jnp.stack(outs[k])
    y_prompt = xp.reshape(BATCH, T_PAD, D_MODEL)[:, N_META:T_REAL]
    y_sample = xs.reshape(DEC_SEQ, DEC_BATCH, D_MODEL).transpose(1, 0, 2)
    return (y_prompt, y_sample,
            st("ckv_p"), st("kpe_p"), st("ckv_s"), st("kpe_s"),
            st("s5r_p"), st("s5i_p"), st("s5r_s"), st("s5i_s"),
            st("mc_p"), st("mn_p"), st("mm_p"), st("mc_s"), st("mn_s"), st("mm_s"),
            st("conv_p"), st("conv_s"))
```

```python
import functools
import math

import jax
import jax.numpy as jnp
from jax import lax
from jax.experimental import pallas as pl
from jax.experimental.pallas import tpu as pltpu

BF = jnp.bfloat16
F32 = jnp.float32

D_MODEL = 2048
BATCH = 2
SEQ = 4096
DEPTH = 4
DEC_BATCH = 128
DEC_SEQ = 4
N_META = 16
S5_WIDTH = 1024
S5_GROUP = 16
S5_GROUPS = 64
S5_STATE = 64
ML_WIDTH = 1024
ML_HEADS = 8
ML_HEAD_DIM = 128
MLA_HEADS = 16
Q_LORA = 512
KV_LORA = 512
QK_NOPE = 128
QK_ROPE = 64
V_DIM = 128
ROPE_THETA = 10000.0
MLA_SCALE = (QK_NOPE + QK_ROPE) ** -0.5
D_FF = 5632
ALPHA = (2 * DEPTH) ** 0.25
LN_EPS = 1e-5
RMS_EPS = 1e-6
NEG = -1e30

LANES = 128
SUBLANES = 8
BF16_ROWS = 16
VMEM_LIMIT = 56 * 1024 * 1024

T_REAL = N_META + SEQ
T_PAD = 4224
ROWS_P = BATCH * T_PAD
ROWS_S = DEC_BATCH * DEC_SEQ
S5_BLK = 512
N_S5_BLK = S5_GROUPS * S5_STATE // S5_BLK
PAGES_PER_STEP = 16
PAGES_PER_GROUP = 16


def _cp(sem, vmem=VMEM_LIMIT):
    return pltpu.CompilerParams(dimension_semantics=sem, vmem_limit_bytes=vmem)


def _dot(a, b):
    return jnp.dot(a, b, preferred_element_type=F32)


def _dot_nt(a, b):
    return lax.dot_general(a, b, (((1,), (1,)), ((), ())), preferred_element_type=F32)


def _sigmoid(x):
    return 1.0 / (1.0 + jnp.exp(-x))


def _layer_norm(xf, g, b):
    mu = jnp.mean(xf, axis=-1, keepdims=True)
    xc = xf - mu
    var = jnp.mean(xc * xc, axis=-1, keepdims=True)
    return xc * lax.rsqrt(var + LN_EPS) * g + b


def _mm_body(x_ref, w_ref, o_ref):
    o_ref[...] = _dot(x_ref[...].astype(BF), w_ref[...]).astype(o_ref.dtype)


def _mm(x, w, layer, out_dtype, tm, tn):
    rows, k = x.shape
    n = w.shape[-1]
    return pl.pallas_call(
        _mm_body,
        grid=(rows // tm, n // tn),
        in_specs=[pl.BlockSpec((tm, k), lambda i, j: (i, 0)),
                  pl.BlockSpec((None, k, tn), lambda i, j: (layer, 0, j))],
        out_specs=pl.BlockSpec((tm, tn), lambda i, j: (i, j)),
        out_shape=jax.ShapeDtypeStruct((rows, n), out_dtype),
        compiler_params=_cp(("parallel", "arbitrary")),
        name="mm",
    )(x, w)


def _mix_proj_body(x_ref, w_ref, wg_ref, p_ref, g_ref, xb_s):
    @pl.when(pl.program_id(1) == 0)
    def _():
        xb = x_ref[...].astype(BF)
        xb_s[...] = xb
        g_ref[...] = _dot(xb, wg_ref[...])

    p_ref[...] = _dot(xb_s[...], w_ref[...])


def _mix_proj(x, w_in_b, w_g_b, layer, tm):
    rows = x.shape[0]
    tn = 512
    n_main = S5_WIDTH + 4 * ML_WIDTH
    return pl.pallas_call(
        _mix_proj_body,
        grid=(rows // tm, n_main // tn),
        in_specs=[pl.BlockSpec((tm, D_MODEL), lambda i, j: (i, 0)),
                  pl.BlockSpec((None, D_MODEL, tn), lambda i, j: (layer, 0, j)),
                  pl.BlockSpec((None, D_MODEL, LANES), lambda i, j: (layer, 0, 0))],
        out_specs=[pl.BlockSpec((tm, tn), lambda i, j: (i, j)),
                   pl.BlockSpec((tm, LANES), lambda i, j: (i, 0))],
        out_shape=[jax.ShapeDtypeStruct((rows, n_main), F32),
                   jax.ShapeDtypeStruct((rows, LANES), F32)],
        scratch_shapes=[pltpu.VMEM((tm, D_MODEL), BF)],
        compiler_params=_cp(("parallel", "arbitrary")),
        name="mix_proj",
    )(x, w_in_b, w_g_b)


def _s5_params_body(are_ref, aim_ref, ldt_ref, bre_ref, bim_ref, abr_ref, abi_ref, bbr_ref, bbi_ref):
    lam_re = are_ref[...]
    lam_im = aim_ref[...]
    dt = jnp.exp(ldt_ref[...])
    mag = jnp.exp(lam_re * dt)
    ab_re = mag * jnp.cos(lam_im * dt)
    ab_im = mag * jnp.sin(lam_im * dt)
    den = lam_re * lam_re + lam_im * lam_im
    z_re = ((ab_re - 1.0) * lam_re + ab_im * lam_im) / den
    z_im = (ab_im * lam_re - (ab_re - 1.0) * lam_im) / den
    br = bre_ref[...]
    bi = bim_ref[...]
    abr_ref[...] = ab_re
    abi_ref[...] = ab_im
    bbr_ref[...] = z_re * br - z_im * bi
    bbi_ref[...] = z_re * bi + z_im * br


def _s5_params(a_re, a_im, log_dt, b_re, b_im):
    n = S5_GROUPS * S5_STATE
    col = lambda a: a.reshape(n, 1)
    ldt = jnp.broadcast_to(log_dt[:, None], (S5_GROUPS, S5_STATE)).reshape(n, 1)
    out = pl.pallas_call(
        _s5_params_body,
        out_shape=[jax.ShapeDtypeStruct((n, 1), F32), jax.ShapeDtypeStruct((n, 1), F32),
                   jax.ShapeDtypeStruct((n, S5_GROUP), F32), jax.ShapeDtypeStruct((n, S5_GROUP), F32)],
        name="s5_params",
    )(col(a_re), col(a_im), ldt, b_re.reshape(n, S5_GROUP), b_im.reshape(n, S5_GROUP))
    ab_re, ab_im, bb_re, bb_im = out
    return ab_re.reshape(1, n), ab_im.reshape(1, n), bb_re, bb_im


def _blockdiag_in(bb):
    nb = N_S5_BLK
    g = S5_GROUPS // nb
    b4 = bb.reshape(nb, g, S5_STATE, S5_GROUP).transpose(0, 1, 3, 2)
    eye = jnp.eye(g, dtype=bool)[None, :, None, :, None]
    out = jnp.where(eye, b4[:, :, :, None, :], 0.0)
    return out.reshape(nb, g * S5_GROUP, g * S5_STATE).astype(BF)


def _blockdiag_out(c):
    nb = N_S5_BLK
    g = S5_GROUPS // nb
    c4 = c.reshape(nb, g, S5_GROUP, S5_STATE).transpose(0, 1, 3, 2)
    eye = jnp.eye(g, dtype=bool)[None, :, None, :, None]
    out = jnp.where(eye, c4[:, :, :, None, :], 0.0)
    return out.reshape(nb, g * S5_STATE, g * S5_GROUP).astype(BF)


def _cmul(ar, ai, br, bi):
    return ar * br - ai * bi, ar * bi + ai * br


def _s5_prompt_body(u_ref, bre_ref, bim_ref, cre_ref, cim_ref, d_ref, ar_ref, ai_ref,
                    y_ref, hre_ref, him_ref, xr_s, xi_s, car_s, cai_s, *, tc, t_final):
    c = pl.program_id(2)

    @pl.when(c == 0)
    def _():
        car_s[...] = jnp.zeros_like(car_s)
        cai_s[...] = jnp.zeros_like(cai_s)

    u = u_ref[...]
    ub = u.astype(BF)
    xr_s[...] = _dot(ub, bre_ref[...])
    xi_s[...] = _dot(ub, bim_ref[...])

    a1 = (ar_ref[...], ai_ref[...])
    a2 = _cmul(*a1, *a1)
    a3 = _cmul(*a2, *a1)
    a4 = _cmul(*a2, *a2)
    a5 = _cmul(*a4, *a1)
    a6 = _cmul(*a4, *a2)
    a7 = _cmul(*a4, *a3)
    a8 = _cmul(*a4, *a4)
    pows = (a1, a2, a3, a4, a5, a6, a7, a8)
    pw_r = jnp.concatenate([p[0] for p in pows], axis=0)
    pw_i = jnp.concatenate([p[1] for p in pows], axis=0)
    sub = lax.broadcasted_iota(jnp.int32, (SUBLANES, S5_BLK), 0)

    def tile(n, carry):
        cr, ci = carry
        off = pl.multiple_of(n * SUBLANES, SUBLANES)
        xr = xr_s[pl.ds(off, SUBLANES), :]
        xi = xi_s[pl.ds(off, SUBLANES), :]
        for d, (pr, pi) in ((1, a1), (2, a2), (4, a4)):
            keep = sub >= d
            sr = jnp.where(keep, pltpu.roll(xr, d, 0), 0.0)
            si = jnp.where(keep, pltpu.roll(xi, d, 0), 0.0)
            xr, xi = xr + pr * sr - pi * si, xi + pr * si + pi * sr
        hr = xr + pw_r * cr - pw_i * ci
        hi = xi + pw_r * ci + pw_i * cr
        xr_s[pl.ds(off, SUBLANES), :] = hr
        xi_s[pl.ds(off, SUBLANES), :] = hi
        return hr[SUBLANES - 1:SUBLANES, :], hi[SUBLANES - 1:SUBLANES, :]

    cr, ci = lax.fori_loop(0, tc // SUBLANES, tile, (car_s[...], cai_s[...]), unroll=4)
    car_s[...] = cr
    cai_s[...] = ci

    hr = xr_s[...]
    hi = xi_s[...]
    y_ref[...] = _dot(hr.astype(BF), cre_ref[...]) - _dot(hi.astype(BF), cim_ref[...]) + d_ref[...] * u

    @pl.when(c == t_final // tc)
    def _():
        r = t_final % tc
        hre_ref[...] = jnp.broadcast_to(xr_s[r:r + 1, :], hre_ref.shape)
        him_ref[...] = jnp.broadcast_to(xi_s[r:r + 1, :], him_ref.shape)


def _s5_prompt(proj, bre, bim, cre, cim, d_row, ab_re, ab_im):
    tc = 384
    nt = T_PAD // tc
    body = functools.partial(_s5_prompt_body, tc=tc, t_final=T_REAL - 1)
    wspec_in = pl.BlockSpec((None, LANES, S5_BLK), lambda b, s, c: (s, 0, 0))
    wspec_out = pl.BlockSpec((None, S5_BLK, LANES), lambda b, s, c: (s, 0, 0))
    aspec = pl.BlockSpec((1, S5_BLK), lambda b, s, c: (0, s))
    st_spec = pl.BlockSpec((None, None, SUBLANES, S5_BLK), lambda b, s, c: (b, s, 0, 0))
    st_shape = jax.ShapeDtypeStruct((BATCH, N_S5_BLK, SUBLANES, S5_BLK), F32)
    y, hre, him = pl.pallas_call(
        body,
        grid=(BATCH, N_S5_BLK, nt),
        in_specs=[pl.BlockSpec((tc, LANES), lambda b, s, c: (b * nt + c, s)),
                  wspec_in, wspec_in, wspec_out, wspec_out,
                  pl.BlockSpec((1, LANES), lambda b, s, c: (0, s)),
                  aspec, aspec],
        out_specs=[pl.BlockSpec((tc, LANES), lambda b, s, c: (b * nt + c, s)), st_spec, st_spec],
        out_shape=[jax.ShapeDtypeStruct((ROWS_P, S5_WIDTH), F32), st_shape, st_shape],
        scratch_shapes=[pltpu.VMEM((tc, S5_BLK), F32), pltpu.VMEM((tc, S5_BLK), F32),
                        pltpu.VMEM((1, S5_BLK), F32), pltpu.VMEM((1, S5_BLK), F32)],
        compiler_params=_cp(("parallel", "parallel", "arbitrary")),
        name="s5_prompt",
    )(proj, bre, bim, cre, cim, d_row, ab_re, ab_im)
    fin = lambda h: h[:, :, 0, :].reshape(BATCH, S5_GROUPS, S5_STATE)
    return y, fin(hre), fin(him)


def _s5_sample_body(u_ref, h0r_ref, h0i_ref, bre_ref, bim_ref, cre_ref, cim_ref, d_ref, ar_ref, ai_ref,
                    y_ref, hre_ref, him_ref):
    u = u_ref[...]
    ub = u.astype(BF)
    bur = _dot(ub, bre_ref[...])
    bui = _dot(ub, bim_ref[...])
    ar = ar_ref[...]
    ai = ai_ref[...]
    hr = h0r_ref[...]
    hi = h0i_ref[...]
    hrs, his = [], []
    for t in range(DEC_SEQ):
        lo = t * DEC_BATCH
        hr, hi = (ar * hr - ai * hi + bur[lo:lo + DEC_BATCH], ar * hi + ai * hr + bui[lo:lo + DEC_BATCH])
        hrs.append(hr)
        his.append(hi)
    hra = jnp.concatenate(hrs, axis=0)
    hia = jnp.concatenate(his, axis=0)
    y_ref[...] = _dot(hra.astype(BF), cre_ref[...]) - _dot(hia.astype(BF), cim_ref[...]) + d_ref[...] * u
    hre_ref[...] = hr
    him_ref[...] = hi


def _s5_sample(proj, h0_re, h0_im, bre, bim, cre, cim, d_row, ab_re, ab_im):
    n = S5_GROUPS * S5_STATE
    wspec_in = pl.BlockSpec((None, LANES, S5_BLK), lambda s: (s, 0, 0))
    wspec_out = pl.BlockSpec((None, S5_BLK, LANES), lambda s: (s, 0, 0))
    aspec = pl.BlockSpec((1, S5_BLK), lambda s: (0, s))
    hspec = pl.BlockSpec((DEC_BATCH, S5_BLK), lambda s: (0, s))
    y, hre, him = pl.pallas_call(
        _s5_sample_body,
        grid=(N_S5_BLK,),
        in_specs=[pl.BlockSpec((ROWS_S, LANES), lambda s: (0, s)), hspec, hspec,
                  wspec_in, wspec_in, wspec_out, wspec_out,
                  pl.BlockSpec((1, LANES), lambda s: (0, s)), aspec, aspec],
        out_specs=[pl.BlockSpec((ROWS_S, LANES), lambda s: (0, s)), hspec, hspec],
        out_shape=[jax.ShapeDtypeStruct((ROWS_S, S5_WIDTH), F32),
                   jax.ShapeDtypeStruct((DEC_BATCH, n), F32), jax.ShapeDtypeStruct((DEC_BATCH, n), F32)],
        compiler_params=_cp(("parallel",)),
        name="s5_sample",
    )(proj, h0_re.reshape(DEC_BATCH, n), h0_im.reshape(DEC_BATCH, n), bre, bim, cre, cim, d_row, ab_re, ab_im)
    return y, hre.reshape(DEC_BATCH, S5_GROUPS, S5_STATE), him.reshape(DEC_BATCH, S5_GROUPS, S5_STATE)


def _glu_body(y_ref, w_ref, b_ref, o_ref):
    y = y_ref[...]
    z = 0.5 * y * (1.0 + jnp.tanh(math.sqrt(2.0 / math.pi) * (y + 0.044715 * (y * y * y))))
    o_ref[...] = (z * _sigmoid(_dot(z.astype(BF), w_ref[...]) + b_ref[...])).astype(o_ref.dtype)


def _glu(y, w_b, b_row, layer, tm):
    rows = y.shape[0]
    return pl.pallas_call(
        _glu_body,
        grid=(rows // tm,),
        in_specs=[pl.BlockSpec((tm, S5_WIDTH), lambda i: (i, 0)),
                  pl.BlockSpec((None, S5_WIDTH, S5_WIDTH), lambda i: (layer, 0, 0)),
                  pl.BlockSpec((1, S5_WIDTH), lambda i: (0, 0))],
        out_specs=pl.BlockSpec((tm, S5_WIDTH), lambda i: (i, 0)),
        out_shape=jax.ShapeDtypeStruct((rows, S5_WIDTH), BF),
        compiler_params=_cp(("parallel",)),
        name="s5_glu",
    )(y, w_b, b_row)


def _log_sigmoid(x):
    return jnp.minimum(x, 0.0) - jnp.log1p(jnp.exp(-jnp.abs(x)))


def _split_dot(tri_b, x):
    hi = x.astype(BF)
    r1 = x - hi.astype(F32)
    mid = r1.astype(BF)
    lo = (r1 - mid.astype(F32)).astype(BF)
    return _dot(tri_b, hi) + _dot(tri_b, mid) + _dot(tri_b, lo)


def _mlstm_body(*refs, chunk, n_valid, rows_in, has_init):
    q_ref, k_ref, v_ref, og_ref, g_ref, bias_ref, ng_ref = refs[:7]
    pos = 7
    if has_init:
        c0_ref, n0_ref, m0_ref = refs[pos:pos + 3]
        pos += 3
    y_ref, c_ref, n_ref, m_ref = refs[pos:pos + 4]
    pads = refs[pos + 4:]
    c = pl.program_id(1)
    L = chunk

    @pl.when(c == 0)
    def _():
        if has_init:
            c_ref[...] = c0_ref[...]
            n_ref[...] = n0_ref[...]
            m_ref[...] = m0_ref[...]
        else:
            c_ref[...] = jnp.zeros_like(c_ref)
            n_ref[...] = jnp.zeros_like(n_ref)
            m_ref[...] = jnp.zeros_like(m_ref)

    if rows_in < L:
        @pl.when(pl.program_id(0) == 0)
        def _():
            for p in pads:
                p[...] = jnp.zeros_like(p)

        for src, p in zip((q_ref, k_ref, v_ref, og_ref, g_ref), pads):
            p[0:rows_in, :] = src[...]
        q_src, k_src, v_src, og_src, g_src = pads
    else:
        q_src, k_src, v_src, og_src, g_src = q_ref, k_ref, v_ref, og_ref, g_ref

    rid = lax.broadcasted_iota(jnp.int32, (L, 1), 0) + c * L
    valid = rid < n_valid
    lane = lax.broadcasted_iota(jnp.int32, (1, LANES), 1)
    g = g_src[...] + bias_ref[...]
    lf = jnp.where(valid, _log_sigmoid(g), 0.0)
    x = jnp.where(lane < ML_HEADS, jnp.where(valid, g, NEG), lf)
    ti = lax.broadcasted_iota(jnp.int32, (L, L), 0)
    si = lax.broadcasted_iota(jnp.int32, (L, L), 1)
    causal = si <= ti
    tri_b = jnp.where(causal, 1.0, 0.0).astype(BF)
    bc = _split_dot(tri_b, lf)
    xt = x.T
    bt = bc.T
    scale = ML_HEAD_DIM ** -0.5

    for h in range(ML_HEADS):
        sl = slice(h * ML_HEAD_DIM, (h + 1) * ML_HEAD_DIM)
        q = q_src[:, sl]
        k = k_src[:, sl] * scale
        v = v_src[:, sl]
        ig_col = x[:, h:h + 1]
        ig_row = xt[h:h + 1, :]
        b_col = bc[:, ML_HEADS + h:ML_HEADS + h + 1]
        b_row = bt[ML_HEADS + h:ML_HEADS + h + 1, :]
        m_prev = m_ref[h:h + 1, 0:1]
        cmat = c_ref[h]
        n_row = n_ref[h:h + 1, :]

        log_d = jnp.where(causal, b_col - b_row + ig_row, NEG)
        m_inter = b_col + m_prev
        m_t = jnp.maximum(m_inter, jnp.max(log_d, axis=1, keepdims=True))
        dmat = jnp.exp(log_d - m_t)
        w_inter = jnp.exp(m_inter - m_t)
        qb = q.astype(BF)
        kb = k.astype(BF)
        vb = v.astype(BF)
        s = _dot_nt(qb, kb) * dmat
        num = w_inter * _dot_nt(qb, cmat.astype(BF)) + _dot(s.astype(BF), vb)
        den = w_inter * jnp.sum(q * n_row, axis=1, keepdims=True) + jnp.sum(s, axis=1, keepdims=True)
        hh = num / jnp.maximum(jnp.abs(den), jnp.exp(-m_t))
        mu = jnp.mean(hh, axis=1, keepdims=True)
        hc = hh - mu
        var = jnp.mean(hc * hc, axis=1, keepdims=True)
        hn = hc * lax.rsqrt(var + LN_EPS) * ng_ref[:, sl]
        yv = (_sigmoid(og_src[:, sl]) * hn).astype(y_ref.dtype)
        if rows_in < L:
            y_ref[:, sl] = yv[0:rows_in, :]
        else:
            y_ref[:, sl] = yv

        m_new = m_t[L - 1:L, :]
        w_s = jnp.exp(b_col[L - 1:L, :] - b_col + ig_col - m_new)
        f_s = jnp.exp(m_inter[L - 1:L, :] - m_new)
        vw_t = (v * w_s).T
        c_ref[h] = f_s * cmat + _dot(vw_t.astype(BF), kb)
        n_ref[h:h + 1, :] = f_s * n_row + jnp.sum(k * w_s, axis=0, keepdims=True)
        m_ref[h:h + 1, :] = jnp.broadcast_to(m_new, (1, LANES))


def _mlstm_prompt(proj, gates, bias_row, ng_row):
    L = 128
    nc = T_PAD // L
    body = functools.partial(_mlstm_body, chunk=L, n_valid=T_REAL, rows_in=L, has_init=False)
    col = lambda off: pl.BlockSpec((L, ML_WIDTH), lambda b, c: (b * nc + c, off))
    st = lambda *tail: pl.BlockSpec((None,) + tail, lambda b, c: (b,) + (0,) * len(tail))
    y, cst, nst, mst = pl.pallas_call(
        body,
        grid=(BATCH, nc),
        in_specs=[col(1), col(2), col(3), col(4),
                  pl.BlockSpec((L, LANES), lambda b, c: (b * nc + c, 0)),
                  pl.BlockSpec((1, LANES), lambda b, c: (0, 0)),
                  pl.BlockSpec((1, ML_WIDTH), lambda b, c: (0, 0))],
        out_specs=[pl.BlockSpec((L, ML_WIDTH), lambda b, c: (b * nc + c, 0)),
                   st(ML_HEADS, ML_HEAD_DIM, ML_HEAD_DIM), st(ML_HEADS, ML_HEAD_DIM), st(ML_HEADS, LANES)],
        out_shape=[jax.ShapeDtypeStruct((ROWS_P, ML_WIDTH), BF),
                   jax.ShapeDtypeStruct((BATCH, ML_HEADS, ML_HEAD_DIM, ML_HEAD_DIM), F32),
                   jax.ShapeDtypeStruct((BATCH, ML_HEADS, ML_HEAD_DIM), F32),
                   jax.ShapeDtypeStruct((BATCH, ML_HEADS, LANES), F32)],
        compiler_params=_cp(("parallel", "arbitrary")),
        name="mlstm_prompt",
    )(proj, proj, proj, proj, gates, bias_row, ng_row)
    return y, cst, nst, mst[:, :, 0]


def _mlstm_sample(proj_bm, gates_bm, bias_row, ng_row, c0, n0, m0, layer):
    L = 128
    body = functools.partial(_mlstm_body, chunk=L, n_valid=DEC_SEQ, rows_in=DEC_SEQ, has_init=True)
    col = lambda off: pl.BlockSpec((None, DEC_SEQ, ML_WIDTH), lambda b, c: (b, 0, off))
    st_in = lambda *tail: pl.BlockSpec((None, None) + tail, lambda b, c: (layer, b) + (0,) * len(tail))
    st = lambda *tail: pl.BlockSpec((None,) + tail, lambda b, c: (b,) + (0,) * len(tail))
    m0b = jnp.broadcast_to(m0[..., None], m0.shape + (LANES,))
    y, cst, nst, mst = pl.pallas_call(
        body,
        grid=(DEC_BATCH, 1),
        in_specs=[col(1), col(2), col(3), col(4),
                  pl.BlockSpec((None, DEC_SEQ, LANES), lambda b, c: (b, 0, 0)),
                  pl.BlockSpec((1, LANES), lambda b, c: (0, 0)),
                  pl.BlockSpec((1, ML_WIDTH), lambda b, c: (0, 0)),
                  st_in(ML_HEADS, ML_HEAD_DIM, ML_HEAD_DIM), st_in(ML_HEADS, ML_HEAD_DIM), st_in(ML_HEADS, LANES)],
        out_specs=[pl.BlockSpec((None, DEC_SEQ, ML_WIDTH), lambda b, c: (b, 0, 0)),
                   st(ML_HEADS, ML_HEAD_DIM, ML_HEAD_DIM), st(ML_HEADS, ML_HEAD_DIM), st(ML_HEADS, LANES)],
        out_shape=[jax.ShapeDtypeStruct((DEC_BATCH, DEC_SEQ, ML_WIDTH), BF),
                   jax.ShapeDtypeStruct((DEC_BATCH, ML_HEADS, ML_HEAD_DIM, ML_HEAD_DIM), F32),
                   jax.ShapeDtypeStruct((DEC_BATCH, ML_HEADS, ML_HEAD_DIM), F32),
                   jax.ShapeDtypeStruct((DEC_BATCH, ML_HEADS, LANES), F32)],
        scratch_shapes=[pltpu.VMEM((L, ML_WIDTH), F32)] * 4 + [pltpu.VMEM((L, LANES), F32)],
        compiler_params=_cp(("arbitrary", "arbitrary")),
        name="mlstm_sample",
    )(proj_bm, proj_bm, proj_bm, proj_bm, gates_bm, bias_row, ng_row, c0, n0, m0b)
    return y, cst, nst, mst[:, :, 0]


def _proj_ln_body(*refs, n_in):
    a_refs = refs[:n_in]
    w_refs = refs[n_in:2 * n_in]
    x_ref, g_ref, b_ref, o_ref = refs[2 * n_in:]
    y = _dot(a_refs[0][...].astype(BF), w_refs[0][...])
    for a_ref, w_ref in zip(a_refs[1:], w_refs[1:]):
        y = y + _dot(a_ref[...].astype(BF), w_ref[...])
    o_ref[...] = _layer_norm(ALPHA * x_ref[...] + y, g_ref[...], b_ref[...])


def _proj_ln(acts, w_b, layer, x, g_row, b_row, tm):
    rows = x.shape[0]
    n_in = len(acts)
    a_specs, w_specs = [], []
    for a in acts:
        kk = a.shape[1]
        slab = len(w_specs)
        a_specs.append(pl.BlockSpec((tm, kk), lambda i: (i, 0)))
        w_specs.append(pl.BlockSpec((None, kk, D_MODEL), lambda i, slab=slab: (layer, slab, 0)))
    row_spec = pl.BlockSpec((1, D_MODEL), lambda i: (0, 0))
    return pl.pallas_call(
        functools.partial(_proj_ln_body, n_in=n_in),
        grid=(rows // tm,),
        in_specs=a_specs + w_specs + [pl.BlockSpec((tm, D_MODEL), lambda i: (i, 0)), row_spec, row_spec],
        out_specs=pl.BlockSpec((tm, D_MODEL), lambda i: (i, 0)),
        out_shape=jax.ShapeDtypeStruct((rows, D_MODEL), F32),
        compiler_params=_cp(("parallel",)),
        name="proj_ln",
    )(*acts, *([w_b] * n_in), x, g_row, b_row)


HALO = BF16_ROWS


FF_SUB = 256


def _conv3(up, cw, cb, lo, n):
    r1 = pltpu.roll(up, 1, 0)
    r2 = pltpu.roll(up, 2, 0)
    return cb + cw[2:3, :] * up[lo:lo + n] + cw[1:2, :] * r1[lo:lo + n] + cw[0:1, :] * r2[lo:lo + n]


def _ffn_prompt_body(x_ref, xh_ref, wv_ref, wg_ref, cwv_ref, cwg_ref, cbv_ref, cbg_ref, wd_ref,
                     g_ref, b_ref, o_ref, xb_s, acc_s, *, tm, tiles_per_seq):
    i = pl.program_id(0)
    j = pl.program_id(1)

    @pl.when(j == 0)
    def _():
        first = (i % tiles_per_seq) == 0
        halo = jnp.where(first, 0.0, xh_ref[...])
        xb_s[0:HALO, :] = halo.astype(BF)
        xb_s[HALO:, :] = x_ref[...].astype(BF)
        acc_s[...] = jnp.zeros_like(acc_s)

    xb = xb_s[...]
    acts = []
    for c in range(wv_ref.shape[1] // FF_SUB):
        cs = slice(c * FF_SUB, (c + 1) * FF_SUB)
        hv = _conv3(_dot(xb, wv_ref[:, cs]), cwv_ref[:, cs], cbv_ref[:, cs], HALO, tm)
        hg = _conv3(_dot(xb, wg_ref[:, cs]), cwg_ref[:, cs], cbg_ref[:, cs], HALO, tm)
        acts.append((hg * _sigmoid(hg) * hv).astype(BF))
    acc_s[...] += _dot(jnp.concatenate(acts, axis=1), wd_ref[...])

    @pl.when(j == pl.num_programs(1) - 1)
    def _():
        o_ref[...] = _layer_norm(ALPHA * x_ref[...] + acc_s[...], g_ref[...], b_ref[...])


def _ffn_prompt(x, w_up_b, conv_w, conv_b, w_down_b, layer, g_row, b_row):
    tm, tf = 704, 512
    nf = D_FF // tf
    tiles_per_seq = T_PAD // tm
    body = functools.partial(_ffn_prompt_body, tm=tm, tiles_per_seq=tiles_per_seq)
    halo_blocks = tm // HALO
    row_spec = pl.BlockSpec((1, D_MODEL), lambda i, j: (0, 0))
    return pl.pallas_call(
        body,
        grid=(ROWS_P // tm, nf),
        in_specs=[pl.BlockSpec((tm, D_MODEL), lambda i, j: (i, 0)),
                  pl.BlockSpec((HALO, D_MODEL), lambda i, j: (jnp.maximum(i * halo_blocks - 1, 0), 0)),
                  pl.BlockSpec((None, D_MODEL, tf), lambda i, j: (layer, 0, j)),
                  pl.BlockSpec((None, D_MODEL, tf), lambda i, j: (layer, 0, j + nf)),
                  pl.BlockSpec((None, 3, tf), lambda i, j: (layer, 0, j)),
                  pl.BlockSpec((None, 3, tf), lambda i, j: (layer, 0, j + nf)),
                  pl.BlockSpec((None, 1, tf), lambda i, j: (layer, 0, j)),
                  pl.BlockSpec((None, 1, tf), lambda i, j: (layer, 0, j + nf)),
                  pl.BlockSpec((None, tf, D_MODEL), lambda i, j: (layer, j, 0)),
                  row_spec, row_spec],
        out_specs=pl.BlockSpec((tm, D_MODEL), lambda i, j: (i, 0)),
        out_shape=jax.ShapeDtypeStruct((ROWS_P, D_MODEL), F32),
        scratch_shapes=[pltpu.VMEM((tm + HALO, D_MODEL), BF), pltpu.VMEM((tm, D_MODEL), F32)],
        compiler_params=_cp(("parallel", "arbitrary")),
        name="ffn_prompt",
    )(x, x, w_up_b, w_up_b, conv_w, conv_w, conv_b, conv_b, w_down_b, g_row, b_row)


EXTRA = SUBLANES


def _ffn_sample_body(x_ref, p0v_ref, p0g_ref, p1v_ref, p1g_ref, wv_ref, wg_ref, cwv_ref, cwg_ref,
                     cbv_ref, cbg_ref, wd_ref, g_ref, b_ref, o_ref, cs_ref, cp_ref, xb_s, acc_s):
    j = pl.program_id(0)

    @pl.when(j == 0)
    def _():
        xb_s[...] = x_ref[...].astype(BF)
        acc_s[...] = jnp.zeros_like(acc_s)

    xb = xb_s[...]
    n = ROWS_S
    bsz = DEC_BATCH

    def conv(up, p0_ref, p1_ref, cw_ref, cb_ref):
        cw = cw_ref[...]
        ext = jnp.concatenate([p0_ref[...], p1_ref[...], up[0:n]], axis=0)
        return (cb_ref[...] + cw[0:1, :] * ext[0:n] + cw[1:2, :] * ext[bsz:bsz + n]
                + cw[2:3, :] * ext[2 * bsz:2 * bsz + n])

    upv = _dot(xb, wv_ref[...])
    upg = _dot(xb, wg_ref[...])
    hv = conv(upv, p0v_ref, p1v_ref, cwv_ref, cbv_ref)
    hg = conv(upg, p0g_ref, p1g_ref, cwg_ref, cbg_ref)
    act = hg * _sigmoid(hg) * hv
    acc_s[...] += _dot(act.astype(BF), wd_ref[...])
    lo = (DEC_SEQ - 2) * bsz
    cs_ref[0] = upv[lo:lo + bsz]
    cs_ref[1] = upg[lo:lo + bsz]
    cs_ref[2] = upv[lo + bsz:lo + 2 * bsz]
    cs_ref[3] = upg[lo + bsz:lo + 2 * bsz]
    cp_ref[0] = upv[n:n + EXTRA]
    cp_ref[1] = upg[n:n + EXTRA]

    @pl.when(j == pl.num_programs(0) - 1)
    def _():
        o_ref[...] = _layer_norm(ALPHA * x_ref[0:n, :] + acc_s[...], g_ref[...], b_ref[...])


def _ffn_sample(x_ext, prev, w_up_b, conv_w, conv_b, w_down_b, layer, g_row, b_row):
    tf = 512
    nf = D_FF // tf
    rows = ROWS_S + EXTRA
    prev_spec = lambda q: pl.BlockSpec((None, DEC_BATCH, tf), lambda j: (layer, 0, j + q * nf))
    row_spec = pl.BlockSpec((1, D_MODEL), lambda j: (0, 0))
    return pl.pallas_call(
        _ffn_sample_body,
        grid=(nf,),
        in_specs=[pl.BlockSpec((rows, D_MODEL), lambda j: (0, 0)),
                  prev_spec(0), prev_spec(1), prev_spec(2), prev_spec(3),
                  pl.BlockSpec((None, D_MODEL, tf), lambda j: (layer, 0, j)),
                  pl.BlockSpec((None, D_MODEL, tf), lambda j: (layer, 0, j + nf)),
                  pl.BlockSpec((None, 3, tf), lambda j: (layer, 0, j)),
                  pl.BlockSpec((None, 3, tf), lambda j: (layer, 0, j + nf)),
                  pl.BlockSpec((None, 1, tf), lambda j: (layer, 0, j)),
                  pl.BlockSpec((None, 1, tf), lambda j: (layer, 0, j + nf)),
                  pl.BlockSpec((None, tf, D_MODEL), lambda j: (layer, j, 0)),
                  row_spec, row_spec],
        out_specs=[pl.BlockSpec((ROWS_S, D_MODEL), lambda j: (0, 0)),
                   pl.BlockSpec((4, DEC_BATCH, tf), lambda j: (0, 0, j)),
                   pl.BlockSpec((2, EXTRA, tf), lambda j: (0, 0, j))],
        out_shape=[jax.ShapeDtypeStruct((ROWS_S, D_MODEL), F32),
                   jax.ShapeDtypeStruct((4, DEC_BATCH, D_FF), F32),
                   jax.ShapeDtypeStruct((2, EXTRA, D_FF), F32)],
        scratch_shapes=[pltpu.VMEM((rows, D_MODEL), BF), pltpu.VMEM((ROWS_S, D_MODEL), F32)],
        compiler_params=_cp(("arbitrary",)),
        name="ffn_sample",
    )(x_ext, prev, prev, prev, prev, w_up_b, w_up_b, conv_w, conv_w, conv_b, conv_b, w_down_b, g_row, b_row)


def _rms(xf, g):
    return xf * lax.rsqrt(jnp.mean(xf * xf, axis=-1, keepdims=True) + RMS_EPS) * g


def _mla_in_body(x_ref, win_ref, gq_ref, gkv_ref, wuq_ref, cos_ref, sin_ref,
                 ckv_ref, kk_ref, qn_ref, qp_ref):
    p = _dot(x_ref[...].astype(BF), win_ref[...])
    cq = _rms(p[:, 0:Q_LORA], gq_ref[...])
    ckv_ref[...] = _rms(p[:, Q_LORA:Q_LORA + KV_LORA], gkv_ref[...])
    cos = cos_ref[...]
    sin = sin_ref[...]
    o = Q_LORA + KV_LORA
    kk_ref[...] = p[:, o:o + LANES] * cos + p[:, o + LANES:o + 2 * LANES] * sin
    q = _dot(cq.astype(BF), wuq_ref[...])
    n_nope = MLA_HEADS * QK_NOPE
    n_pe = MLA_HEADS * QK_ROPE
    qn_ref[...] = q[:, 0:n_nope].astype(BF)
    reps = n_pe // LANES
    cos_w = jnp.concatenate([cos] * reps, axis=1)
    sin_w = jnp.concatenate([sin] * reps, axis=1)
    qp_ref[...] = (q[:, n_nope:n_nope + n_pe] * cos_w + q[:, n_nope + n_pe:] * sin_w).astype(BF)


def _mla_in(x, win_b, gq_row, gkv_row, wuq_b, cos_t, sin_t, layer, tm):
    rows = x.shape[0]
    n_in = win_b.shape[-1]
    n_q = wuq_b.shape[-1]
    row = lambda w: pl.BlockSpec((tm, w), lambda i: (i, 0))
    return pl.pallas_call(
        _mla_in_body,
        grid=(rows // tm,),
        in_specs=[row(D_MODEL),
                  pl.BlockSpec((None, D_MODEL, n_in), lambda i: (layer, 0, 0)),
                  pl.BlockSpec((1, Q_LORA), lambda i: (0, 0)),
                  pl.BlockSpec((1, KV_LORA), lambda i: (0, 0)),
                  pl.BlockSpec((None, Q_LORA, n_q), lambda i: (layer, 0, 0)),
                  row(LANES), row(LANES)],
        out_specs=[row(KV_LORA), row(LANES), row(MLA_HEADS * QK_NOPE), row(MLA_HEADS * QK_ROPE)],
        out_shape=[jax.ShapeDtypeStruct((rows, KV_LORA), F32),
                   jax.ShapeDtypeStruct((rows, LANES), F32),
                   jax.ShapeDtypeStruct((rows, MLA_HEADS * QK_NOPE), BF),
                   jax.ShapeDtypeStruct((rows, MLA_HEADS * QK_ROPE), BF)],
        compiler_params=_cp(("parallel",)),
        name="mla_in",
    )(x, win_b, gq_row, gkv_row, wuq_b, cos_t, sin_t)


def _fold(x, op):
    out = x[:, 0:LANES]
    for c in range(1, x.shape[1] // LANES):
        out = op(out, x[:, c * LANES:(c + 1) * LANES])
    return out


def _flash_body(qn_ref, qp_ref, kn_ref, kk_ref, v_ref, o_ref, s_s, mx_s, l_s, acc_s, *, tq):
    qi = pl.program_id(2)
    lane = lax.broadcasted_iota(jnp.int32, (1, LANES), 1)
    qp = qp_ref[...].astype(F32)
    qn = qn_ref[...]
    half = LANES // 2
    qs = (jnp.concatenate([qn[:, 0:QK_NOPE], jnp.where(lane < half, qp, 0.0).astype(BF)], axis=1),
          jnp.concatenate([qn[:, QK_NOPE:], jnp.where(lane >= half, qp, 0.0).astype(BF)], axis=1))
    mx_s[...] = jnp.full_like(mx_s, NEG)
    l_s[...] = jnp.zeros_like(l_s)
    acc_s[...] = jnp.zeros_like(acc_s)
    reps = tq // LANES

    def logits(kc, masked):
        off = pl.multiple_of(kc * tq, tq)
        kn = kn_ref[pl.ds(off, tq), :]
        kk = kk_ref[pl.ds(off, tq), :].astype(BF)
        for hh in range(2):
            kh = jnp.concatenate([kn[:, hh * QK_NOPE:(hh + 1) * QK_NOPE], kk], axis=1)
            s = _dot_nt(qs[hh], kh) * MLA_SCALE
            if masked:
                ti = lax.broadcasted_iota(jnp.int32, (tq, tq), 0)
                si = lax.broadcasted_iota(jnp.int32, (tq, tq), 1)
                s = jnp.where(si <= ti, s, NEG)
            s_s[hh, kc] = s
            mx_s[hh] = jnp.maximum(mx_s[hh], _fold(s, jnp.maximum))

    def pass1(kc, carry):
        logits(kc, False)
        return carry

    lax.fori_loop(0, qi, pass1, 0)
    logits(qi, True)

    m_rows = [jnp.concatenate([jnp.broadcast_to(jnp.max(mx_s[hh], axis=1, keepdims=True), (tq, LANES))] * reps,
                              axis=1) for hh in range(2)]

    def pass2(kc, carry):
        off = pl.multiple_of(kc * tq, tq)
        vv = v_ref[pl.ds(off, tq), :]
        for hh in range(2):
            p = jnp.exp(s_s[hh, kc] - m_rows[hh])
            l_s[hh] += _fold(p, jnp.add)
            acc_s[hh] += _dot(p.astype(BF), vv[:, hh * V_DIM:(hh + 1) * V_DIM])
        return carry

    lax.fori_loop(0, qi + 1, pass2, 0)
    outs = [acc_s[hh] / jnp.sum(l_s[hh], axis=1, keepdims=True) for hh in range(2)]
    o_ref[...] = jnp.concatenate(outs, axis=1).astype(o_ref.dtype)


def _flash(qn, qp, kn, kk, v):
    tq = 384
    nq = T_PAD // tq
    body = functools.partial(_flash_body, tq=tq)
    two = 2 * QK_NOPE
    return pl.pallas_call(
        body,
        grid=(BATCH, MLA_HEADS // 2, nq),
        in_specs=[pl.BlockSpec((tq, two), lambda b, h, q: (b * nq + q, h)),
                  pl.BlockSpec((tq, LANES), lambda b, h, q: (b * nq + q, h)),
                  pl.BlockSpec((T_PAD, two), lambda b, h, q: (b, h)),
                  pl.BlockSpec((T_PAD, LANES), lambda b, h, q: (b, 0)),
                  pl.BlockSpec((T_PAD, two), lambda b, h, q: (b, h))],
        out_specs=pl.BlockSpec((tq, two), lambda b, h, q: (b * nq + q, h)),
        out_shape=jax.ShapeDtypeStruct((ROWS_P, MLA_HEADS * V_DIM), BF),
        scratch_shapes=[pltpu.VMEM((2, nq, tq, tq), F32), pltpu.VMEM((2, tq, LANES), F32),
                        pltpu.VMEM((2, tq, LANES), F32), pltpu.VMEM((2, tq, V_DIM), F32)],
        compiler_params=_cp(("parallel", "parallel", "arbitrary")),
        name="mla_flash",
    )(qn, qp, kn, kk, v)


def _absorb_body(qn_ref, wuk_ref, o_ref):
    o_ref[...] = _dot_nt(qn_ref[...], wuk_ref[...]).astype(o_ref.dtype)


def _absorb(qn, wuk_b, layer):
    return pl.pallas_call(
        _absorb_body,
        grid=(MLA_HEADS,),
        in_specs=[pl.BlockSpec((ROWS_S, QK_NOPE), lambda h: (0, h)),
                  pl.BlockSpec((None, KV_LORA, QK_NOPE), lambda h: (layer, 0, h))],
        out_specs=pl.BlockSpec((ROWS_S, KV_LORA), lambda h: (0, h)),
        out_shape=jax.ShapeDtypeStruct((ROWS_S, MLA_HEADS * KV_LORA), BF),
        compiler_params=_cp(("parallel",)),
        name="mla_absorb",
    )(qn, wuk_b)


def _unabsorb_body(ol_ref, wuv_ref, o_ref):
    o_ref[...] = _dot(ol_ref[...], wuv_ref[...]).astype(o_ref.dtype)


def _unabsorb(ol, wuv_b, layer):
    return pl.pallas_call(
        _unabsorb_body,
        grid=(MLA_HEADS,),
        in_specs=[pl.BlockSpec((ROWS_S, KV_LORA), lambda h: (0, h)),
                  pl.BlockSpec((None, KV_LORA, V_DIM), lambda h: (layer, 0, h))],
        out_specs=pl.BlockSpec((ROWS_S, V_DIM), lambda h: (0, h)),
        out_shape=jax.ShapeDtypeStruct((ROWS_S, MLA_HEADS * V_DIM), BF),
        compiler_params=_cp(("parallel",)),
        name="mla_unabsorb",
    )(ol, wuv_b)


QROWS = DEC_SEQ * MLA_HEADS


def _paged_body(pt_ref, ql_ref, qp_ref, cn_ref, kn_ref, *rest, n_pages, group):
    ckv_refs = rest[:n_pages]
    kpe_refs = rest[n_pages:2 * n_pages]
    o_ref, m_s, l_s, acc_s, newc_s, newk_s = rest[2 * n_pages:]
    j = pl.program_id(1)

    @pl.when(j == 0)
    def _():
        m_s[...] = jnp.full_like(m_s, NEG)
        l_s[...] = jnp.zeros_like(l_s)
        acc_s[...] = jnp.zeros_like(acc_s)

    ql = ql_ref[...].reshape(QROWS, KV_LORA)
    qp = qp_ref[...].reshape(QROWS, QK_ROPE)

    def local_softmax(s, vb):
        m = jnp.max(s, axis=1, keepdims=True)
        p = jnp.exp(s - m)
        return m, jnp.sum(p, axis=1, keepdims=True), _dot(p.astype(BF), vb)

    def merge(parts):
        m_prev = m_s[...]
        m_new = m_prev
        for m, _, _ in parts:
            m_new = jnp.maximum(m_new, m)
        alpha = jnp.exp(m_prev - m_new)
        l = alpha * l_s[...]
        acc = alpha * acc_s[...]
        for m, lg, ag in parts:
            w = jnp.exp(m - m_new)
            l = l + w * lg
            acc = acc + w * ag
        m_s[...] = m_new
        l_s[...] = l
        acc_s[...] = acc

    parts = []
    for g in range(0, n_pages, group):
        cb = jnp.concatenate([r[...].astype(BF) for r in ckv_refs[g:g + group]], axis=0)
        kt = jnp.concatenate([r[...].astype(BF) for r in kpe_refs[g:g + group]], axis=1)
        parts.append(local_softmax((_dot_nt(ql, cb) + _dot(qp, kt)) * MLA_SCALE, cb))
    merge(parts)

    @pl.when(j == pl.num_programs(1) - 1)
    def _():
        newc_s[...] = jnp.zeros_like(newc_s)
        newk_s[...] = jnp.zeros_like(newk_s)
        newc_s[0:DEC_SEQ, :] = cn_ref[...]
        newk_s[0:DEC_SEQ, :] = kn_ref[...]
        cn = newc_s[...].astype(BF)
        s = (_dot_nt(ql, cn) + _dot_nt(qp, newk_s[...].astype(BF))) * MLA_SCALE
        qt = lax.broadcasted_iota(jnp.int32, (QROWS, LANES), 0) // MLA_HEADS
        kt = lax.broadcasted_iota(jnp.int32, (QROWS, LANES), 1)
        merge([local_softmax(jnp.where(kt <= qt, s, NEG), cn)])
        o = acc_s[...] / l_s[...]
        o_ref[...] = o.reshape(DEC_SEQ, MLA_HEADS, KV_LORA).astype(o_ref.dtype)


def _paged_attention(page_table, q_lat, q_pe, ckv_new, kpe_new, cache_ckv, cache_kpe_t, layer):
    n_pages = PAGES_PER_STEP
    n_steps = page_table.shape[1] // n_pages
    page = cache_ckv.shape[2]
    body = functools.partial(_paged_body, n_pages=n_pages, group=PAGES_PER_GROUP)
    qspec = lambda w: pl.BlockSpec((DEC_SEQ, None, MLA_HEADS, w), lambda b, j, pt: (0, b, 0, 0))
    nspec = lambda w: pl.BlockSpec((None, DEC_SEQ, w), lambda b, j, pt: (b, 0, 0))

    def pspec(shape, i):
        return pl.BlockSpec((None, None) + shape,
                            lambda b, j, pt, i=i: (layer, pt[b, j * n_pages + i], 0, 0))

    grid_spec = pltpu.PrefetchScalarGridSpec(
        num_scalar_prefetch=1,
        grid=(DEC_BATCH, n_steps),
        in_specs=[qspec(KV_LORA), qspec(QK_ROPE), nspec(KV_LORA), nspec(QK_ROPE)]
                 + [pspec((page, KV_LORA), i) for i in range(n_pages)]
                 + [pspec((QK_ROPE, page), i) for i in range(n_pages)],
        out_specs=pl.BlockSpec((DEC_SEQ, None, MLA_HEADS, KV_LORA), lambda b, j, pt: (0, b, 0, 0)),
        scratch_shapes=[pltpu.VMEM((QROWS, 1), F32), pltpu.VMEM((QROWS, 1), F32),
                        pltpu.VMEM((QROWS, KV_LORA), F32),
                        pltpu.VMEM((LANES, KV_LORA), F32), pltpu.VMEM((LANES, QK_ROPE), F32)],
    )
    return pl.pallas_call(
        body,
        grid_spec=grid_spec,
        out_shape=jax.ShapeDtypeStruct((DEC_SEQ, DEC_BATCH, MLA_HEADS, KV_LORA), BF),
        compiler_params=_cp(("parallel", "arbitrary")),
        name="mla_paged",
    )(page_table, q_lat, q_pe, ckv_new, kpe_new, *([cache_ckv] * n_pages), *([cache_kpe_t] * n_pages))


def _rope_tables(pos):
    half = QK_ROPE // 2
    inv = ROPE_THETA ** (-jnp.arange(half, dtype=F32) / half)
    ang = pos.astype(F32)[:, None] * inv[None, :]
    cos, sin = jnp.cos(ang), jnp.sin(ang)
    return (jnp.concatenate([cos, cos, cos, cos], axis=1),
            jnp.concatenate([-sin, sin, -sin, sin], axis=1))


def _swap_halves(w):
    half = w.shape[-1] // 2
    return jnp.concatenate([w[..., half:], w[..., :half]], axis=-1)


def kernel(x_prompt, x_sample, cache_mla_ckv, cache_mla_kpe, page_table, state_s5_re, state_s5_im, state_mlstm_c, state_mlstm_n, state_mlstm_m, state_ffn_conv, meta_tokens, ln1_g, ln1_b, ln2_g, ln2_b, mix_w_in, mix_b_gates, s5_a_re, s5_a_im, s5_log_dt, s5_b_re, s5_b_im, s5_c_re, s5_c_im, s5_d, s5_w_glu, s5_b_glu, ml_norm_g, mix_w_out, mla_w_in, mla_q_norm_g, mla_kv_norm_g, mla_w_uq, mla_w_uk, mla_w_uv, mla_w_out, ffn_w_up, ffn_conv_w, ffn_conv_b, ffn_w_down):
    n_mix = mix_w_in.shape[0]
    n_mla = mla_w_in.shape[0]
    past = page_table.shape[1] * cache_mla_ckv.shape[2]

    n_main = S5_WIDTH + 4 * ML_WIDTH
    mix_w_in_b = mix_w_in.astype(BF)
    mix_w_g_b = jnp.pad(mix_w_in[:, :, n_main:], ((0, 0), (0, 0), (0, LANES - 2 * ML_HEADS))).astype(BF)
    gate_bias = jnp.pad(mix_b_gates, ((0, 0), (0, LANES - 2 * ML_HEADS)))
    w_glu_b = s5_w_glu.astype(BF)
    mix_w_out_b = mix_w_out.astype(BF)
    o = Q_LORA + KV_LORA
    w_kpe = mla_w_in[:, :, o:]
    w_kpe_sw = _swap_halves(w_kpe)
    mla_w_in_b = jnp.concatenate([mla_w_in[:, :, :o], w_kpe, w_kpe, w_kpe_sw, w_kpe_sw], axis=-1).astype(BF)
    wq4 = mla_w_uq.reshape(n_mla, Q_LORA, MLA_HEADS, QK_NOPE + QK_ROPE)
    wq_pe = wq4[..., QK_NOPE:]
    mla_w_uq_b = jnp.concatenate(
        [wq4[..., :QK_NOPE].reshape(n_mla, Q_LORA, -1), wq_pe.reshape(n_mla, Q_LORA, -1),
         _swap_halves(wq_pe).reshape(n_mla, Q_LORA, -1)], axis=-1).astype(BF)
    mla_w_uk_b = mla_w_uk.reshape(n_mla, KV_LORA, MLA_HEADS * QK_NOPE).astype(BF)
    mla_w_uv_b = mla_w_uv.reshape(n_mla, KV_LORA, MLA_HEADS * V_DIM).astype(BF)
    mla_w_out_b = mla_w_out.astype(BF)
    ffn_w_up_b = ffn_w_up.astype(BF)
    ffn_w_down_b = ffn_w_down.astype(BF)
    conv_b3 = ffn_conv_b[:, None, :]
    prev_conv = state_ffn_conv.reshape(DEPTH, DEC_BATCH, 4 * D_FF)
    cache_kpe_t = jnp.swapaxes(cache_mla_kpe, 2, 3)
    cos_p, sin_p = _rope_tables(jnp.arange(T_PAD))
    cos_p = jnp.concatenate([cos_p] * BATCH, axis=0)
    sin_p = jnp.concatenate([sin_p] * BATCH, axis=0)
    cos_s, sin_s = _rope_tables(past + jnp.arange(DEC_SEQ))
    cos_s = jnp.repeat(cos_s, DEC_BATCH, axis=0)
    sin_s = jnp.repeat(sin_s, DEC_BATCH, axis=0)

    xp = jnp.concatenate([jnp.broadcast_to(meta_tokens[None], (BATCH, N_META, D_MODEL)), x_prompt], axis=1)
    xp = jnp.pad(xp, ((0, 0), (0, T_PAD - T_REAL), (0, 0))).reshape(ROWS_P, D_MODEL)
    xs = x_sample.transpose(1, 0, 2).reshape(ROWS_S, D_MODEL)
    tm_p, tm_s = 704, ROWS_S

    outs = {k: [] for k in ("ckv_p", "kpe_p", "ckv_s", "kpe_s", "s5r_p", "s5i_p", "s5r_s", "s5i_s",
                            "mc_p", "mn_p", "mm_p", "mc_s", "mn_s", "mm_s", "conv_p", "conv_s")}

    for l in range(DEPTH):
        j = l // 2
        g1, b1 = ln1_g[l][None, :], ln1_b[l][None, :]
        g2, b2 = ln2_g[l][None, :], ln2_b[l][None, :]
        if l % 2 == 0:
            ab_re, ab_im, bb_re, bb_im = _s5_params(s5_a_re[j], s5_a_im[j], s5_log_dt[j], s5_b_re[j], s5_b_im[j])
            bre, bim = _blockdiag_in(bb_re), _blockdiag_in(bb_im)
            cre, cim = _blockdiag_out(s5_c_re[j]), _blockdiag_out(s5_c_im[j])
            d_row = s5_d[j].reshape(1, S5_WIDTH)
            bias_row = gate_bias[j][None, :]
            ng_row = ml_norm_g[j][None, :]
            bglu_row = s5_b_glu[j][None, :]

            proj, gates = _mix_proj(xp, mix_w_in_b, mix_w_g_b, j, tm_p)
            y5, hr, hi = _s5_prompt(proj, bre, bim, cre, cim, d_row, ab_re, ab_im)
            y5 = _glu(y5, w_glu_b, bglu_row, j, tm_p)
            yml, c1, n1, m1 = _mlstm_prompt(proj, gates, bias_row, ng_row)
            xp = _proj_ln([y5, yml], mix_w_out_b, j, xp, g1, b1, 352)
            outs["s5r_p"].append(hr); outs["s5i_p"].append(hi)
            outs["mc_p"].append(c1); outs["mn_p"].append(n1); outs["mm_p"].append(m1)

            proj, gates = _mix_proj(xs, mix_w_in_b, mix_w_g_b, j, tm_s)
            y5, hr, hi = _s5_sample(proj, state_s5_re[j], state_s5_im[j], bre, bim, cre, cim, d_row, ab_re, ab_im)
            y5 = _glu(y5, w_glu_b, bglu_row, j, tm_s)
            proj_bm = proj.reshape(DEC_SEQ, DEC_BATCH, n_main).transpose(1, 0, 2)
            gates_bm = gates.reshape(DEC_SEQ, DEC_BATCH, LANES).transpose(1, 0, 2)
            yml, c1, n1, m1 = _mlstm_sample(proj_bm, gates_bm, bias_row, ng_row,
                                            state_mlstm_c, state_mlstm_n, state_mlstm_m, j)
            yml = yml.transpose(1, 0, 2).reshape(ROWS_S, ML_WIDTH)
            xs = _proj_ln([y5, yml], mix_w_out_b, j, xs, g1, b1, 256)
            outs["s5r_s"].append(hr); outs["s5i_s"].append(hi)
            outs["mc_s"].append(c1); outs["mn_s"].append(n1); outs["mm_s"].append(m1)
        else:
            gq_row = mla_q_norm_g[j][None, :]
            gkv_row = mla_kv_norm_g[j][None, :]

            ckv, kk, qn, qp = _mla_in(xp, mla_w_in_b, gq_row, gkv_row, mla_w_uq_b, cos_p, sin_p, j, 352)
            kn = _mm(ckv, mla_w_uk_b, j, BF, 384, MLA_HEADS * QK_NOPE)
            vv = _mm(ckv, mla_w_uv_b, j, BF, 384, MLA_HEADS * V_DIM)
            att = _flash(qn, qp, kn, kk, vv)
            xp = _proj_ln([att], mla_w_out_b, j, xp, g1, b1, 352)
            outs["ckv_p"].append(ckv.reshape(BATCH, T_PAD, KV_LORA)[:, :T_REAL])
            outs["kpe_p"].append(kk.reshape(BATCH, T_PAD, LANES)[:, :T_REAL, :QK_ROPE])

            ckv, kk, qn, qp = _mla_in(xs, mla_w_in_b, gq_row, gkv_row, mla_w_uq_b, cos_s, sin_s, j, 256)
            q_lat = _absorb(qn, mla_w_uk_b, j)
            ckv_bm = ckv.reshape(DEC_SEQ, DEC_BATCH, KV_LORA).transpose(1, 0, 2)
            kpe_bm = kk[:, :QK_ROPE].reshape(DEC_SEQ, DEC_BATCH, QK_ROPE).transpose(1, 0, 2)
            o_lat = _paged_attention(
                page_table,
                q_lat.reshape(DEC_SEQ, DEC_BATCH, MLA_HEADS, KV_LORA),
                qp.reshape(DEC_SEQ, DEC_BATCH, MLA_HEADS, QK_ROPE),
                ckv_bm, kpe_bm, cache_mla_ckv, cache_kpe_t, j)
            att = _unabsorb(o_lat.reshape(ROWS_S, MLA_HEADS * KV_LORA), mla_w_uv_b, j)
            xs = _proj_ln([att], mla_w_out_b, j, xs, g1, b1, 256)
            outs["ckv_s"].append(ckv_bm)
            outs["kpe_s"].append(kpe_bm)

        tail = jnp.concatenate([xp[b * T_PAD + T_REAL - 2:b * T_PAD + T_REAL] for b in range(BATCH)]
                               + [jnp.zeros((EXTRA - 2 * BATCH, D_MODEL), F32)], axis=0)
        xs_ext = jnp.concatenate([xs, tail], axis=0)
        xs, cs, cpv = _ffn_sample(xs_ext, prev_conv, ffn_w_up_b, ffn_conv_w, conv_b3, ffn_w_down_b, l, g2, b2)
        xp = _ffn_prompt(xp, ffn_w_up_b, ffn_conv_w, conv_b3, ffn_w_down_b, l, g2, b2)
        outs["conv_s"].append(cs.transpose(1, 0, 2).reshape(DEC_BATCH, 2, 2 * D_FF))
        cp2 = jnp.concatenate([cpv[0, :2 * BATCH], cpv[1, :2 * BATCH]], axis=-1)
        outs["conv_p"].append(cp2.reshape(BATCH, 2, 2 * D_FF))

    st = lambda k: jnp.stack(outs[k])
    y_prompt = xp.reshape(BATCH, T_PAD, D_MODEL)[:, N_META:T_REAL]
    y_sample = xs.reshape(DEC_SEQ, DEC_BATCH, D_MODEL).transpose(1, 0, 2)
    return (y_prompt, y_sample,
            st("ckv_p"), st("kpe_p"), st("ckv_s"), st("kpe_s"),
            st("s5r_p"), st("s5i_p"), st("s5r_s"), st("s5i_s"),
            st("mc_p"), st("mn_p"), st("mm_p"), st("mc_s"), st("mn_s"), st("mm_s"),
            st("conv_p"), st("conv_s"))
```

```python
import functools
import math

import jax
import jax.numpy as jnp
from jax import lax
from jax.experimental import pallas as pl
from jax.experimental.pallas import tpu as pltpu

BF = jnp.bfloat16
F32 = jnp.float32

D_MODEL = 2048
BATCH = 2
SEQ = 4096
DEPTH = 4
DEC_BATCH = 128
DEC_SEQ = 4
N_META = 16
S5_WIDTH = 1024
S5_GROUP = 16
S5_GROUPS = 64
S5_STATE = 64
ML_WIDTH = 1024
ML_HEADS = 8
ML_HEAD_DIM = 128
MLA_HEADS = 16
Q_LORA = 512
KV_LORA = 512
QK_NOPE = 128
QK_ROPE = 64
V_DIM = 128
ROPE_THETA = 10000.0
MLA_SCALE = (QK_NOPE + QK_ROPE) ** -0.5
D_FF = 5632
ALPHA = (2 * DEPTH) ** 0.25
LN_EPS = 1e-5
RMS_EPS = 1e-6
NEG = -1e30
LOG2E = 1.4426950408889634

LANES = 128
SUBLANES = 8
BF16_ROWS = 16
VMEM_LIMIT = 56 * 1024 * 1024

T_REAL = N_META + SEQ
T_PAD = 4224
ROWS_P = BATCH * T_PAD
ROWS_S = DEC_BATCH * DEC_SEQ
S5_BLK = 512
N_S5_BLK = S5_GROUPS * S5_STATE // S5_BLK
PAGES_PER_STEP = 16


def _cp(sem, vmem=VMEM_LIMIT):
    return pltpu.CompilerParams(dimension_semantics=sem, vmem_limit_bytes=vmem)


def _dot(a, b):
    return jnp.dot(a, b, preferred_element_type=F32)


def _dot_nt(a, b):
    return lax.dot_general(a, b, (((1,), (1,)), ((), ())), preferred_element_type=F32)


def _sigmoid(x):
    return 1.0 / (1.0 + jnp.exp(-x))


def _layer_norm(xf, g, b):
    mu = jnp.mean(xf, axis=-1, keepdims=True)
    xc = xf - mu
    var = jnp.mean(xc * xc, axis=-1, keepdims=True)
    return xc * lax.rsqrt(var + LN_EPS) * g + b


def _mm_body(x_ref, w_ref, o_ref):
    o_ref[...] = _dot(x_ref[...].astype(BF), w_ref[...]).astype(o_ref.dtype)


def _mm(x, w, layer, out_dtype, tm, tn):
    rows, k = x.shape
    n = w.shape[-1]
    return pl.pallas_call(
        _mm_body,
        grid=(rows // tm, n // tn),
        in_specs=[pl.BlockSpec((tm, k), lambda i, j: (i, 0)),
                  pl.BlockSpec((None, k, tn), lambda i, j: (layer, 0, j))],
        out_specs=pl.BlockSpec((tm, tn), lambda i, j: (i, j)),
        out_shape=jax.ShapeDtypeStruct((rows, n), out_dtype),
        compiler_params=_cp(("parallel", "arbitrary")),
        name="mm",
    )(x, w)


def _mix_proj_body(x_ref, w_ref, wg_ref, p_ref, g_ref, xb_s):
    @pl.when(pl.program_id(1) == 0)
    def _():
        xb = x_ref[...].astype(BF)
        xb_s[...] = xb
        g_ref[...] = _dot(xb, wg_ref[...])

    p_ref[...] = _dot(xb_s[...], w_ref[...])


def _mix_proj(x, w_in_b, w_g_b, layer, tm):
    rows = x.shape[0]
    tn = 512
    n_main = S5_WIDTH + 4 * ML_WIDTH
    return pl.pallas_call(
        _mix_proj_body,
        grid=(rows // tm, n_main // tn),
        in_specs=[pl.BlockSpec((tm, D_MODEL), lambda i, j: (i, 0)),
                  pl.BlockSpec((None, D_MODEL, tn), lambda i, j: (layer, 0, j)),
                  pl.BlockSpec((None, D_MODEL, LANES), lambda i, j: (layer, 0, 0))],
        out_specs=[pl.BlockSpec((tm, tn), lambda i, j: (i, j)),
                   pl.BlockSpec((tm, LANES), lambda i, j: (i, 0))],
        out_shape=[jax.ShapeDtypeStruct((rows, n_main), F32),
                   jax.ShapeDtypeStruct((rows, LANES), F32)],
        scratch_shapes=[pltpu.VMEM((tm, D_MODEL), BF)],
        compiler_params=_cp(("parallel", "arbitrary")),
        name="mix_proj",
    )(x, w_in_b, w_g_b)


def _s5_params_body(are_ref, aim_ref, ldt_ref, bre_ref, bim_ref, abr_ref, abi_ref, bbr_ref, bbi_ref):
    lam_re = are_ref[...]
    lam_im = aim_ref[...]
    dt = jnp.exp(ldt_ref[...])
    mag = jnp.exp(lam_re * dt)
    ab_re = mag * jnp.cos(lam_im * dt)
    ab_im = mag * jnp.sin(lam_im * dt)
    den = lam_re * lam_re + lam_im * lam_im
    z_re = ((ab_re - 1.0) * lam_re + ab_im * lam_im) / den
    z_im = (ab_im * lam_re - (ab_re - 1.0) * lam_im) / den
    br = bre_ref[...]
    bi = bim_ref[...]
    abr_ref[...] = ab_re
    abi_ref[...] = ab_im
    bbr_ref[...] = z_re * br - z_im * bi
    bbi_ref[...] = z_re * bi + z_im * br


def _s5_params(a_re, a_im, log_dt, b_re, b_im):
    n = S5_GROUPS * S5_STATE
    col = lambda a: a.reshape(n, 1)
    ldt = jnp.broadcast_to(log_dt[:, None], (S5_GROUPS, S5_STATE)).reshape(n, 1)
    out = pl.pallas_call(
        _s5_params_body,
        out_shape=[jax.ShapeDtypeStruct((n, 1), F32), jax.ShapeDtypeStruct((n, 1), F32),
                   jax.ShapeDtypeStruct((n, S5_GROUP), F32), jax.ShapeDtypeStruct((n, S5_GROUP), F32)],
        name="s5_params",
    )(col(a_re), col(a_im), ldt, b_re.reshape(n, S5_GROUP), b_im.reshape(n, S5_GROUP))
    ab_re, ab_im, bb_re, bb_im = out
    return ab_re.reshape(1, n), ab_im.reshape(1, n), bb_re, bb_im


def _blockdiag_in(bb):
    nb = N_S5_BLK
    g = S5_GROUPS // nb
    b4 = bb.reshape(nb, g, S5_STATE, S5_GROUP).transpose(0, 1, 3, 2)
    eye = jnp.eye(g, dtype=bool)[None, :, None, :, None]
    out = jnp.where(eye, b4[:, :, :, None, :], 0.0)
    return out.reshape(nb, g * S5_GROUP, g * S5_STATE).astype(BF)


def _blockdiag_out(c):
    nb = N_S5_BLK
    g = S5_GROUPS // nb
    c4 = c.reshape(nb, g, S5_GROUP, S5_STATE).transpose(0, 1, 3, 2)
    eye = jnp.eye(g, dtype=bool)[None, :, None, :, None]
    out = jnp.where(eye, c4[:, :, :, None, :], 0.0)
    return out.reshape(nb, g * S5_STATE, g * S5_GROUP).astype(BF)


def _cmul(ar, ai, br, bi):
    return ar * br - ai * bi, ar * bi + ai * br


def _s5_prompt_body(u_ref, bre_ref, bim_ref, cre_ref, cim_ref, d_ref, ar_ref, ai_ref,
                    y_ref, hre_ref, him_ref, xr_s, xi_s, car_s, cai_s, *, tc, t_final):
    c = pl.program_id(2)

    @pl.when(c == 0)
    def _():
        car_s[...] = jnp.zeros_like(car_s)
        cai_s[...] = jnp.zeros_like(cai_s)

    u = u_ref[...]
    ub = u.astype(BF)
    xr_s[...] = _dot(ub, bre_ref[...])
    xi_s[...] = _dot(ub, bim_ref[...])

    a1 = (ar_ref[...], ai_ref[...])
    a2 = _cmul(*a1, *a1)
    a3 = _cmul(*a2, *a1)
    a4 = _cmul(*a2, *a2)
    a5 = _cmul(*a4, *a1)
    a6 = _cmul(*a4, *a2)
    a7 = _cmul(*a4, *a3)
    a8 = _cmul(*a4, *a4)
    pows = (a1, a2, a3, a4, a5, a6, a7, a8)
    pw_r = jnp.concatenate([p[0] for p in pows], axis=0)
    pw_i = jnp.concatenate([p[1] for p in pows], axis=0)
    sub = lax.broadcasted_iota(jnp.int32, (SUBLANES, S5_BLK), 0)

    def tile(n, carry):
        cr, ci = carry
        off = pl.multiple_of(n * SUBLANES, SUBLANES)
        xr = xr_s[pl.ds(off, SUBLANES), :]
        xi = xi_s[pl.ds(off, SUBLANES), :]
        for d, (pr, pi) in ((1, a1), (2, a2), (4, a4)):
            keep = sub >= d
            sr = jnp.where(keep, pltpu.roll(xr, d, 0), 0.0)
            si = jnp.where(keep, pltpu.roll(xi, d, 0), 0.0)
            xr, xi = xr + pr * sr - pi * si, xi + pr * si + pi * sr
        hr = xr + pw_r * cr - pw_i * ci
        hi = xi + pw_r * ci + pw_i * cr
        xr_s[pl.ds(off, SUBLANES), :] = hr
        xi_s[pl.ds(off, SUBLANES), :] = hi
        return hr[SUBLANES - 1:SUBLANES, :], hi[SUBLANES - 1:SUBLANES, :]

    cr, ci = lax.fori_loop(0, tc // SUBLANES, tile, (car_s[...], cai_s[...]), unroll=4)
    car_s[...] = cr
    cai_s[...] = ci

    hr = xr_s[...]
    hi = xi_s[...]
    y_ref[...] = _dot(hr.astype(BF), cre_ref[...]) - _dot(hi.astype(BF), cim_ref[...]) + d_ref[...] * u

    @pl.when(c == t_final // tc)
    def _():
        r = t_final % tc
        hre_ref[...] = jnp.broadcast_to(xr_s[r:r + 1, :], hre_ref.shape)
        him_ref[...] = jnp.broadcast_to(xi_s[r:r + 1, :], him_ref.shape)


def _s5_prompt(proj, bre, bim, cre, cim, d_row, ab_re, ab_im):
    tc = 384
    nt = T_PAD // tc
    body = functools.partial(_s5_prompt_body, tc=tc, t_final=T_REAL - 1)
    wspec_in = pl.BlockSpec((None, LANES, S5_BLK), lambda b, s, c: (s, 0, 0))
    wspec_out = pl.BlockSpec((None, S5_BLK, LANES), lambda b, s, c: (s, 0, 0))
    aspec = pl.BlockSpec((1, S5_BLK), lambda b, s, c: (0, s))
    st_spec = pl.BlockSpec((None, None, SUBLANES, S5_BLK), lambda b, s, c: (b, s, 0, 0))
    st_shape = jax.ShapeDtypeStruct((BATCH, N_S5_BLK, SUBLANES, S5_BLK), F32)
    y, hre, him = pl.pallas_call(
        body,
        grid=(BATCH, N_S5_BLK, nt),
        in_specs=[pl.BlockSpec((tc, LANES), lambda b, s, c: (b * nt + c, s)),
                  wspec_in, wspec_in, wspec_out, wspec_out,
                  pl.BlockSpec((1, LANES), lambda b, s, c: (0, s)),
                  aspec, aspec],
        out_specs=[pl.BlockSpec((tc, LANES), lambda b, s, c: (b * nt + c, s)), st_spec, st_spec],
        out_shape=[jax.ShapeDtypeStruct((ROWS_P, S5_WIDTH), F32), st_shape, st_shape],
        scratch_shapes=[pltpu.VMEM((tc, S5_BLK), F32), pltpu.VMEM((tc, S5_BLK), F32),
                        pltpu.VMEM((1, S5_BLK), F32), pltpu.VMEM((1, S5_BLK), F32)],
        compiler_params=_cp(("parallel", "parallel", "arbitrary")),
        name="s5_prompt",
    )(proj, bre, bim, cre, cim, d_row, ab_re, ab_im)
    fin = lambda h: h[:, :, 0, :].reshape(BATCH, S5_GROUPS, S5_STATE)
    return y, fin(hre), fin(him)


def _s5_sample_body(u_ref, h0r_ref, h0i_ref, bre_ref, bim_ref, cre_ref, cim_ref, d_ref, ar_ref, ai_ref,
                    y_ref, hre_ref, him_ref):
    u = u_ref[...]
    ub = u.astype(BF)
    bur = _dot(ub, bre_ref[...])
    bui = _dot(ub, bim_ref[...])
    ar = ar_ref[...]
    ai = ai_ref[...]
    hr = h0r_ref[...]
    hi = h0i_ref[...]
    hrs, his = [], []
    for t in range(DEC_SEQ):
        lo = t * DEC_BATCH
        hr, hi = (ar * hr - ai * hi + bur[lo:lo + DEC_BATCH], ar * hi + ai * hr + bui[lo:lo + DEC_BATCH])
        hrs.append(hr)
        his.append(hi)
    hra = jnp.concatenate(hrs, axis=0)
    hia = jnp.concatenate(his, axis=0)
    y_ref[...] = _dot(hra.astype(BF), cre_ref[...]) - _dot(hia.astype(BF), cim_ref[...]) + d_ref[...] * u
    hre_ref[...] = hr
    him_ref[...] = hi


def _s5_sample(proj, h0_re, h0_im, bre, bim, cre, cim, d_row, ab_re, ab_im):
    n = S5_GROUPS * S5_STATE
    wspec_in = pl.BlockSpec((None, LANES, S5_BLK), lambda s: (s, 0, 0))
    wspec_out = pl.BlockSpec((None, S5_BLK, LANES), lambda s: (s, 0, 0))
    aspec = pl.BlockSpec((1, S5_BLK), lambda s: (0, s))
    hspec = pl.BlockSpec((DEC_BATCH, S5_BLK), lambda s: (0, s))
    y, hre, him = pl.pallas_call(
        _s5_sample_body,
        grid=(N_S5_BLK,),
        in_specs=[pl.BlockSpec((ROWS_S, LANES), lambda s: (0, s)), hspec, hspec,
                  wspec_in, wspec_in, wspec_out, wspec_out,
                  pl.BlockSpec((1, LANES), lambda s: (0, s)), aspec, aspec],
        out_specs=[pl.BlockSpec((ROWS_S, LANES), lambda s: (0, s)), hspec, hspec],
        out_shape=[jax.ShapeDtypeStruct((ROWS_S, S5_WIDTH), F32),
                   jax.ShapeDtypeStruct((DEC_BATCH, n), F32), jax.ShapeDtypeStruct((DEC_BATCH, n), F32)],
        compiler_params=_cp(("parallel",)),
        name="s5_sample",
    )(proj, h0_re.reshape(DEC_BATCH, n), h0_im.reshape(DEC_BATCH, n), bre, bim, cre, cim, d_row, ab_re, ab_im)
    return y, hre.reshape(DEC_BATCH, S5_GROUPS, S5_STATE), him.reshape(DEC_BATCH, S5_GROUPS, S5_STATE)


def _glu_body(y_ref, w_ref, b_ref, o_ref):
    y = y_ref[...]
    z = 0.5 * y * (1.0 + jnp.tanh(math.sqrt(2.0 / math.pi) * (y + 0.044715 * (y * y * y))))
    o_ref[...] = (z * _sigmoid(_dot(z.astype(BF), w_ref[...]) + b_ref[...])).astype(o_ref.dtype)


def _glu(y, w_b, b_row, layer, tm):
    rows = y.shape[0]
    return pl.pallas_call(
        _glu_body,
        grid=(rows // tm,),
        in_specs=[pl.BlockSpec((tm, S5_WIDTH), lambda i: (i, 0)),
                  pl.BlockSpec((None, S5_WIDTH, S5_WIDTH), lambda i: (layer, 0, 0)),
                  pl.BlockSpec((1, S5_WIDTH), lambda i: (0, 0))],
        out_specs=pl.BlockSpec((tm, S5_WIDTH), lambda i: (i, 0)),
        out_shape=jax.ShapeDtypeStruct((rows, S5_WIDTH), BF),
        compiler_params=_cp(("parallel",)),
        name="s5_glu",
    )(y, w_b, b_row)


def _log_sigmoid(x):
    return jnp.minimum(x, 0.0) - jnp.log1p(jnp.exp(-jnp.abs(x)))


def _split_dot(tri_b, x):
    hi = x.astype(BF)
    r1 = x - hi.astype(F32)
    mid = r1.astype(BF)
    lo = (r1 - mid.astype(F32)).astype(BF)
    return _dot(tri_b, hi) + _dot(tri_b, mid) + _dot(tri_b, lo)


def _mlstm_body(*refs, chunk, n_valid, rows_in, has_init):
    q_ref, k_ref, v_ref, og_ref, g_ref, bias_ref, ng_ref = refs[:7]
    pos = 7
    if has_init:
        c0_ref, n0_ref, m0_ref = refs[pos:pos + 3]
        pos += 3
    y_ref, c_ref, n_ref, m_ref = refs[pos:pos + 4]
    pads = refs[pos + 4:]
    c = pl.program_id(1)
    L = chunk

    @pl.when(c == 0)
    def _():
        if has_init:
            c_ref[...] = c0_ref[...]
            n_ref[...] = n0_ref[...]
            m_ref[...] = m0_ref[...]
        else:
            c_ref[...] = jnp.zeros_like(c_ref)
            n_ref[...] = jnp.zeros_like(n_ref)
            m_ref[...] = jnp.zeros_like(m_ref)

    if rows_in < L:
        @pl.when(pl.program_id(0) == 0)
        def _():
            for p in pads:
                p[...] = jnp.zeros_like(p)

        for src, p in zip((q_ref, k_ref, v_ref, og_ref, g_ref), pads):
            p[0:rows_in, :] = src[...]
        q_src, k_src, v_src, og_src, g_src = pads
    else:
        q_src, k_src, v_src, og_src, g_src = q_ref, k_ref, v_ref, og_ref, g_ref

    rid = lax.broadcasted_iota(jnp.int32, (L, 1), 0) + c * L
    valid = rid < n_valid
    lane = lax.broadcasted_iota(jnp.int32, (1, LANES), 1)
    g = g_src[...] + bias_ref[...]
    lf = jnp.where(valid, _log_sigmoid(g), 0.0)
    x = jnp.where(lane < ML_HEADS, jnp.where(valid, g, NEG), lf)
    ti = lax.broadcasted_iota(jnp.int32, (L, L), 0)
    si = lax.broadcasted_iota(jnp.int32, (L, L), 1)
    causal = si <= ti
    tri_b = jnp.where(causal, 1.0, 0.0).astype(BF)
    bc = _split_dot(tri_b, lf)
    xt = x.T
    bt = bc.T
    scale = ML_HEAD_DIM ** -0.5

    for h in range(ML_HEADS):
        sl = slice(h * ML_HEAD_DIM, (h + 1) * ML_HEAD_DIM)
        q = q_src[:, sl]
        k = k_src[:, sl] * scale
        v = v_src[:, sl]
        ig_col = x[:, h:h + 1]
        ig_row = xt[h:h + 1, :]
        b_col = bc[:, ML_HEADS + h:ML_HEADS + h + 1]
        b_row = bt[ML_HEADS + h:ML_HEADS + h + 1, :]
        m_prev = m_ref[h:h + 1, 0:1]
        cmat = c_ref[h]
        n_row = n_ref[h:h + 1, :]

        log_d = jnp.where(causal, b_col - b_row + ig_row, NEG)
        m_inter = b_col + m_prev
        m_t = jnp.maximum(m_inter, jnp.max(log_d, axis=1, keepdims=True))
        dmat = jnp.exp(log_d - m_t)
        w_inter = jnp.exp(m_inter - m_t)
        qb = q.astype(BF)
        kb = k.astype(BF)
        vb = v.astype(BF)
        s = _dot_nt(qb, kb) * dmat
        num = w_inter * _dot_nt(qb, cmat.astype(BF)) + _dot(s.astype(BF), vb)
        den = w_inter * jnp.sum(q * n_row, axis=1, keepdims=True) + jnp.sum(s, axis=1, keepdims=True)
        hh = num / jnp.maximum(jnp.abs(den), jnp.exp(-m_t))
        mu = jnp.mean(hh, axis=1, keepdims=True)
        hc = hh - mu
        var = jnp.mean(hc * hc, axis=1, keepdims=True)
        hn = hc * lax.rsqrt(var + LN_EPS) * ng_ref[:, sl]
        yv = (_sigmoid(og_src[:, sl]) * hn).astype(y_ref.dtype)
        if rows_in < L:
            y_ref[:, sl] = yv[0:rows_in, :]
        else:
            y_ref[:, sl] = yv

        m_new = m_t[L - 1:L, :]
        w_s = jnp.exp(b_col[L - 1:L, :] - b_col + ig_col - m_new)
        f_s = jnp.exp(m_inter[L - 1:L, :] - m_new)
        vw_t = (v * w_s).T
        c_ref[h] = f_s * cmat + _dot(vw_t.astype(BF), kb)
        n_ref[h:h + 1, :] = f_s * n_row + jnp.sum(k * w_s, axis=0, keepdims=True)
        m_ref[h:h + 1, :] = jnp.broadcast_to(m_new, (1, LANES))


def _mlstm_prompt(proj, gates, bias_row, ng_row):
    L = 128
    nc = T_PAD // L
    body = functools.partial(_mlstm_body, chunk=L, n_valid=T_REAL, rows_in=L, has_init=False)
    col = lambda off: pl.BlockSpec((L, ML_WIDTH), lambda b, c: (b * nc + c, off))
    st = lambda *tail: pl.BlockSpec((None,) + tail, lambda b, c: (b,) + (0,) * len(tail))
    y, cst, nst, mst = pl.pallas_call(
        body,
        grid=(BATCH, nc),
        in_specs=[col(1), col(2), col(3), col(4),
                  pl.BlockSpec((L, LANES), lambda b, c: (b * nc + c, 0)),
                  pl.BlockSpec((1, LANES), lambda b, c: (0, 0)),
                  pl.BlockSpec((1, ML_WIDTH), lambda b, c: (0, 0))],
        out_specs=[pl.BlockSpec((L, ML_WIDTH), lambda b, c: (b * nc + c, 0)),
                   st(ML_HEADS, ML_HEAD_DIM, ML_HEAD_DIM), st(ML_HEADS, ML_HEAD_DIM), st(ML_HEADS, LANES)],
        out_shape=[jax.ShapeDtypeStruct((ROWS_P, ML_WIDTH), BF),
                   jax.ShapeDtypeStruct((BATCH, ML_HEADS, ML_HEAD_DIM, ML_HEAD_DIM), F32),
                   jax.ShapeDtypeStruct((BATCH, ML_HEADS, ML_HEAD_DIM), F32),
                   jax.ShapeDtypeStruct((BATCH, ML_HEADS, LANES), F32)],
        compiler_params=_cp(("parallel", "arbitrary")),
        name="mlstm_prompt",
    )(proj, proj, proj, proj, gates, bias_row, ng_row)
    return y, cst, nst, mst[:, :, 0]


def _mlstm_sample(proj_bm, gates_bm, bias_row, ng_row, c0, n0, m0, layer):
    L = 128
    body = functools.partial(_mlstm_body, chunk=L, n_valid=DEC_SEQ, rows_in=DEC_SEQ, has_init=True)
    col = lambda off: pl.BlockSpec((None, DEC_SEQ, ML_WIDTH), lambda b, c: (b, 0, off))
    st_in = lambda *tail: pl.BlockSpec((None, None) + tail, lambda b, c: (layer, b) + (0,) * len(tail))
    st = lambda *tail: pl.BlockSpec((None,) + tail, lambda b, c: (b,) + (0,) * len(tail))
    m0b = jnp.broadcast_to(m0[..., None], m0.shape + (LANES,))
    y, cst, nst, mst = pl.pallas_call(
        body,
        grid=(DEC_BATCH, 1),
        in_specs=[col(1), col(2), col(3), col(4),
                  pl.BlockSpec((None, DEC_SEQ, LANES), lambda b, c: (b, 0, 0)),
                  pl.BlockSpec((1, LANES), lambda b, c: (0, 0)),
                  pl.BlockSpec((1, ML_WIDTH), lambda b, c: (0, 0)),
                  st_in(ML_HEADS, ML_HEAD_DIM, ML_HEAD_DIM), st_in(ML_HEADS, ML_HEAD_DIM), st_in(ML_HEADS, LANES)],
        out_specs=[pl.BlockSpec((None, DEC_SEQ, ML_WIDTH), lambda b, c: (b, 0, 0)),
                   st(ML_HEADS, ML_HEAD_DIM, ML_HEAD_DIM), st(ML_HEADS, ML_HEAD_DIM), st(ML_HEADS, LANES)],
        out_shape=[jax.ShapeDtypeStruct((DEC_BATCH, DEC_SEQ, ML_WIDTH), BF),
                   jax.ShapeDtypeStruct((DEC_BATCH, ML_HEADS, ML_HEAD_DIM, ML_HEAD_DIM), F32),
                   jax.ShapeDtypeStruct((DEC_BATCH, ML_HEADS, ML_HEAD_DIM), F32),
                   jax.ShapeDtypeStruct((DEC_BATCH, ML_HEADS, LANES), F32)],
        scratch_shapes=[pltpu.VMEM((L, ML_WIDTH), F32)] * 4 + [pltpu.VMEM((L, LANES), F32)],
        compiler_params=_cp(("arbitrary", "arbitrary")),
        name="mlstm_sample",
    )(proj_bm, proj_bm, proj_bm, proj_bm, gates_bm, bias_row, ng_row, c0, n0, m0b)
    return y, cst, nst, mst[:, :, 0]


def _proj_ln_body(*refs, n_in):
    a_refs = refs[:n_in]
    w_refs = refs[n_in:2 * n_in]
    x_ref, g_ref, b_ref, o_ref = refs[2 * n_in:]
    y = _dot(a_refs[0][...].astype(BF), w_refs[0][...])
    for a_ref, w_ref in zip(a_refs[1:], w_refs[1:]):
        y = y + _dot(a_ref[...].astype(BF), w_ref[...])
    o_ref[...] = _layer_norm(ALPHA * x_ref[...] + y, g_ref[...], b_ref[...])


def _proj_ln(acts, w_b, layer, x, g_row, b_row, tm):
    rows = x.shape[0]
    n_in = len(acts)
    a_specs, w_specs = [], []
    for a in acts:
        kk = a.shape[1]
        slab = len(w_specs)
        a_specs.append(pl.BlockSpec((tm, kk), lambda i: (i, 0)))
        w_specs.append(pl.BlockSpec((None, kk, D_MODEL), lambda i, slab=slab: (layer, slab, 0)))
    row_spec = pl.BlockSpec((1, D_MODEL), lambda i: (0, 0))
    return pl.pallas_call(
        functools.partial(_proj_ln_body, n_in=n_in),
        grid=(rows // tm,),
        in_specs=a_specs + w_specs + [pl.BlockSpec((tm, D_MODEL), lambda i: (i, 0)), row_spec, row_spec],
        out_specs=pl.BlockSpec((tm, D_MODEL), lambda i: (i, 0)),
        out_shape=jax.ShapeDtypeStruct((rows, D_MODEL), F32),
        compiler_params=_cp(("parallel",)),
        name="proj_ln",
    )(*acts, *([w_b] * n_in), x, g_row, b_row)


HALO = BF16_ROWS


def _conv3(up, cw, cb, lo, n):
    r1 = pltpu.roll(up, 1, 0)
    r2 = pltpu.roll(up, 2, 0)
    return cb + cw[2:3, :] * up[lo:lo + n] + cw[1:2, :] * r1[lo:lo + n] + cw[0:1, :] * r2[lo:lo + n]


def _ffn_prompt_body(x_ref, xh_ref, wv_ref, wg_ref, cwv_ref, cwg_ref, cbv_ref, cbg_ref, wd_ref,
                     g_ref, b_ref, o_ref, xb_s, acc_s, *, tm, tiles_per_seq):
    i = pl.program_id(0)
    j = pl.program_id(1)

    @pl.when(j == 0)
    def _():
        first = (i % tiles_per_seq) == 0
        halo = jnp.where(first, 0.0, xh_ref[...])
        xb_s[0:HALO, :] = halo.astype(BF)
        xb_s[HALO:, :] = x_ref[...].astype(BF)
        acc_s[...] = jnp.zeros_like(acc_s)

    xb = xb_s[...]
    hv = _conv3(_dot(xb, wv_ref[...]), cwv_ref[...], cbv_ref[...], HALO, tm)
    hg = _conv3(_dot(xb, wg_ref[...]), cwg_ref[...], cbg_ref[...], HALO, tm)
    act = hg * _sigmoid(hg) * hv
    acc_s[...] += _dot(act.astype(BF), wd_ref[...])

    @pl.when(j == pl.num_programs(1) - 1)
    def _():
        o_ref[...] = _layer_norm(ALPHA * x_ref[...] + acc_s[...], g_ref[...], b_ref[...])


def _ffn_prompt(x, w_up_b, conv_w, conv_b, w_down_b, layer, g_row, b_row):
    tm, tf = 704, 512
    nf = D_FF // tf
    tiles_per_seq = T_PAD // tm
    body = functools.partial(_ffn_prompt_body, tm=tm, tiles_per_seq=tiles_per_seq)
    halo_blocks = tm // HALO
    row_spec = pl.BlockSpec((1, D_MODEL), lambda i, j: (0, 0))
    return pl.pallas_call(
        body,
        grid=(ROWS_P // tm, nf),
        in_specs=[pl.BlockSpec((tm, D_MODEL), lambda i, j: (i, 0)),
                  pl.BlockSpec((HALO, D_MODEL), lambda i, j: (jnp.maximum(i * halo_blocks - 1, 0), 0)),
                  pl.BlockSpec((None, D_MODEL, tf), lambda i, j: (layer, 0, j)),
                  pl.BlockSpec((None, D_MODEL, tf), lambda i, j: (layer, 0, j + nf)),
                  pl.BlockSpec((None, 3, tf), lambda i, j: (layer, 0, j)),
                  pl.BlockSpec((None, 3, tf), lambda i, j: (layer, 0, j + nf)),
                  pl.BlockSpec((None, 1, tf), lambda i, j: (layer, 0, j)),
                  pl.BlockSpec((None, 1, tf), lambda i, j: (layer, 0, j + nf)),
                  pl.BlockSpec((None, tf, D_MODEL), lambda i, j: (layer, j, 0)),
                  row_spec, row_spec],
        out_specs=pl.BlockSpec((tm, D_MODEL), lambda i, j: (i, 0)),
        out_shape=jax.ShapeDtypeStruct((ROWS_P, D_MODEL), F32),
        scratch_shapes=[pltpu.VMEM((tm + HALO, D_MODEL), BF), pltpu.VMEM((tm, D_MODEL), F32)],
        compiler_params=_cp(("parallel", "arbitrary")),
        name="ffn_prompt",
    )(x, x, w_up_b, w_up_b, conv_w, conv_w, conv_b, conv_b, w_down_b, g_row, b_row)


EXTRA = SUBLANES


def _ffn_sample_body(x_ref, p0v_ref, p0g_ref, p1v_ref, p1g_ref, wv_ref, wg_ref, cwv_ref, cwg_ref,
                     cbv_ref, cbg_ref, wd_ref, g_ref, b_ref, o_ref, cs_ref, cp_ref, xb_s, acc_s):
    j = pl.program_id(0)

    @pl.when(j == 0)
    def _():
        xb_s[...] = x_ref[...].astype(BF)
        acc_s[...] = jnp.zeros_like(acc_s)

    xb = xb_s[...]
    n = ROWS_S
    bsz = DEC_BATCH

    def conv(up, p0_ref, p1_ref, cw_ref, cb_ref):
        cw = cw_ref[...]
        ext = jnp.concatenate([p0_ref[...], p1_ref[...], up[0:n]], axis=0)
        return (cb_ref[...] + cw[0:1, :] * ext[0:n] + cw[1:2, :] * ext[bsz:bsz + n]
                + cw[2:3, :] * ext[2 * bsz:2 * bsz + n])

    upv = _dot(xb, wv_ref[...])
    upg = _dot(xb, wg_ref[...])
    hv = conv(upv, p0v_ref, p1v_ref, cwv_ref, cbv_ref)
    hg = conv(upg, p0g_ref, p1g_ref, cwg_ref, cbg_ref)
    act = hg * _sigmoid(hg) * hv
    acc_s[...] += _dot(act.astype(BF), wd_ref[...])
    lo = (DEC_SEQ - 2) * bsz
    cs_ref[0] = upv[lo:lo + bsz]
    cs_ref[1] = upg[lo:lo + bsz]
    cs_ref[2] = upv[lo + bsz:lo + 2 * bsz]
    cs_ref[3] = upg[lo + bsz:lo + 2 * bsz]
    cp_ref[0] = upv[n:n + EXTRA]
    cp_ref[1] = upg[n:n + EXTRA]

    @pl.when(j == pl.num_programs(0) - 1)
    def _():
        o_ref[...] = _layer_norm(ALPHA * x_ref[0:n, :] + acc_s[...], g_ref[...], b_ref[...])


def _ffn_sample(x_ext, prev, w_up_b, conv_w, conv_b, w_down_b, layer, g_row, b_row):
    tf = 512
    nf = D_FF // tf
    rows = ROWS_S + EXTRA
    prev_spec = lambda q: pl.BlockSpec((None, DEC_BATCH, tf), lambda j: (layer, 0, j + q * nf))
    row_spec = pl.BlockSpec((1, D_MODEL), lambda j: (0, 0))
    return pl.pallas_call(
        _ffn_sample_body,
        grid=(nf,),
        in_specs=[pl.BlockSpec((rows, D_MODEL), lambda j: (0, 0)),
                  prev_spec(0), prev_spec(1), prev_spec(2), prev_spec(3),
                  pl.BlockSpec((None, D_MODEL, tf), lambda j: (layer, 0, j)),
                  pl.BlockSpec((None, D_MODEL, tf), lambda j: (layer, 0, j + nf)),
                  pl.BlockSpec((None, 3, tf), lambda j: (layer, 0, j)),
                  pl.BlockSpec((None, 3, tf), lambda j: (layer, 0, j + nf)),
                  pl.BlockSpec((None, 1, tf), lambda j: (layer, 0, j)),
                  pl.BlockSpec((None, 1, tf), lambda j: (layer, 0, j + nf)),
                  pl.BlockSpec((None, tf, D_MODEL), lambda j: (layer, j, 0)),
                  row_spec, row_spec],
        out_specs=[pl.BlockSpec((ROWS_S, D_MODEL), lambda j: (0, 0)),
                   pl.BlockSpec((4, DEC_BATCH, tf), lambda j: (0, 0, j)),
                   pl.BlockSpec((2, EXTRA, tf), lambda j: (0, 0, j))],
        out_shape=[jax.ShapeDtypeStruct((ROWS_S, D_MODEL), F32),
                   jax.ShapeDtypeStruct((4, DEC_BATCH, D_FF), F32),
                   jax.ShapeDtypeStruct((2, EXTRA, D_FF), F32)],
        scratch_shapes=[pltpu.VMEM((rows, D_MODEL), BF), pltpu.VMEM((ROWS_S, D_MODEL), F32)],
        compiler_params=_cp(("arbitrary",)),
        name="ffn_sample",
    )(x_ext, prev, prev, prev, prev, w_up_b, w_up_b, conv_w, conv_w, conv_b, conv_b, w_down_b, g_row, b_row)


def _rms(xf, g):
    return xf * lax.rsqrt(jnp.mean(xf * xf, axis=-1, keepdims=True) + RMS_EPS) * g


def _mla_in_body(x_ref, win_ref, gq_ref, gkv_ref, wuq_ref, cos_ref, sin_ref,
                 ckv_ref, kk_ref, qn_ref, qp_ref):
    p = _dot(x_ref[...].astype(BF), win_ref[...])
    cq = _rms(p[:, 0:Q_LORA], gq_ref[...])
    ckv_ref[...] = _rms(p[:, Q_LORA:Q_LORA + KV_LORA], gkv_ref[...])
    cos = cos_ref[...]
    sin = sin_ref[...]
    o = Q_LORA + KV_LORA
    kk_ref[...] = p[:, o:o + LANES] * cos + p[:, o + LANES:o + 2 * LANES] * sin
    q = _dot(cq.astype(BF), wuq_ref[...])
    n_nope = MLA_HEADS * QK_NOPE
    n_pe = MLA_HEADS * QK_ROPE
    qn_ref[...] = q[:, 0:n_nope].astype(BF)
    reps = n_pe // LANES
    cos_w = jnp.concatenate([cos] * reps, axis=1)
    sin_w = jnp.concatenate([sin] * reps, axis=1)
    qp_ref[...] = (q[:, n_nope:n_nope + n_pe] * cos_w + q[:, n_nope + n_pe:] * sin_w).astype(BF)


def _mla_in(x, win_b, gq_row, gkv_row, wuq_b, cos_t, sin_t, layer, tm):
    rows = x.shape[0]
    n_in = win_b.shape[-1]
    n_q = wuq_b.shape[-1]
    row = lambda w: pl.BlockSpec((tm, w), lambda i: (i, 0))
    return pl.pallas_call(
        _mla_in_body,
        grid=(rows // tm,),
        in_specs=[row(D_MODEL),
                  pl.BlockSpec((None, D_MODEL, n_in), lambda i: (layer, 0, 0)),
                  pl.BlockSpec((1, Q_LORA), lambda i: (0, 0)),
                  pl.BlockSpec((1, KV_LORA), lambda i: (0, 0)),
                  pl.BlockSpec((None, Q_LORA, n_q), lambda i: (layer, 0, 0)),
                  row(LANES), row(LANES)],
        out_specs=[row(KV_LORA), row(LANES), row(MLA_HEADS * QK_NOPE), row(MLA_HEADS * QK_ROPE)],
        out_shape=[jax.ShapeDtypeStruct((rows, KV_LORA), F32),
                   jax.ShapeDtypeStruct((rows, LANES), F32),
                   jax.ShapeDtypeStruct((rows, MLA_HEADS * QK_NOPE), BF),
                   jax.ShapeDtypeStruct((rows, MLA_HEADS * QK_ROPE), BF)],
        compiler_params=_cp(("parallel",)),
        name="mla_in",
    )(x, win_b, gq_row, gkv_row, wuq_b, cos_t, sin_t)


def _fold(x, op):
    out = x[:, 0:LANES]
    for c in range(1, x.shape[1] // LANES):
        out = op(out, x[:, c * LANES:(c + 1) * LANES])
    return out


def _flash_body(qn_ref, qp_ref, kn_ref, kk_ref, v_ref, o_ref, s_s, mx_s, l_s, acc_s, *, tq):
    qi = pl.program_id(2)
    lane = lax.broadcasted_iota(jnp.int32, (1, LANES), 1)
    qp = qp_ref[...].astype(F32)
    qn = qn_ref[...]
    half = LANES // 2
    qs = (jnp.concatenate([qn[:, 0:QK_NOPE], jnp.where(lane < half, qp, 0.0).astype(BF)], axis=1),
          jnp.concatenate([qn[:, QK_NOPE:], jnp.where(lane >= half, qp, 0.0).astype(BF)], axis=1))
    mx_s[...] = jnp.full_like(mx_s, NEG)
    l_s[...] = jnp.zeros_like(l_s)
    acc_s[...] = jnp.zeros_like(acc_s)
    reps = tq // LANES

    def logits(kc, masked):
        off = pl.multiple_of(kc * tq, tq)
        kn = kn_ref[pl.ds(off, tq), :]
        kk = kk_ref[pl.ds(off, tq), :].astype(BF)
        for hh in range(2):
            kh = jnp.concatenate([kn[:, hh * QK_NOPE:(hh + 1) * QK_NOPE], kk], axis=1)
            s = _dot_nt(qs[hh], kh) * (MLA_SCALE * LOG2E)
            if masked:
                ti = lax.broadcasted_iota(jnp.int32, (tq, tq), 0)
                si = lax.broadcasted_iota(jnp.int32, (tq, tq), 1)
                s = jnp.where(si <= ti, s, NEG)
            s_s[hh, kc] = s
            mx_s[hh] = jnp.maximum(mx_s[hh], _fold(s, jnp.maximum))

    def pairs(n, fn):
        def two(p, carry):
            fn(2 * p)
            fn(2 * p + 1)
            return carry

        lax.fori_loop(0, n // 2, two, 0)

        @pl.when(n % 2 == 1)
        def _():
            fn(n - 1)

    pairs(qi, lambda kc: logits(kc, False))
    logits(qi, True)

    m_rows = [jnp.concatenate([jnp.broadcast_to(jnp.max(mx_s[hh], axis=1, keepdims=True), (tq, LANES))] * reps,
                              axis=1) for hh in range(2)]

    def weighted(kc):
        off = pl.multiple_of(kc * tq, tq)
        vv = v_ref[pl.ds(off, tq), :]
        for hh in range(2):
            p = jnp.exp2(s_s[hh, kc] - m_rows[hh])
            l_s[hh] += _fold(p, jnp.add)
            acc_s[hh] += _dot(p.astype(BF), vv[:, hh * V_DIM:(hh + 1) * V_DIM])

    pairs(qi + 1, weighted)
    outs = [acc_s[hh] / jnp.sum(l_s[hh], axis=1, keepdims=True) for hh in range(2)]
    o_ref[...] = jnp.concatenate(outs, axis=1).astype(o_ref.dtype)


def _flash(qn, qp, kn, kk, v):
    tq = 384
    nq = T_PAD // tq
    body = functools.partial(_flash_body, tq=tq)
    two = 2 * QK_NOPE
    return pl.pallas_call(
        body,
        grid=(BATCH, MLA_HEADS // 2, nq),
        in_specs=[pl.BlockSpec((tq, two), lambda b, h, q: (b * nq + q, h)),
                  pl.BlockSpec((tq, LANES), lambda b, h, q: (b * nq + q, h)),
                  pl.BlockSpec((T_PAD, two), lambda b, h, q: (b, h)),
                  pl.BlockSpec((T_PAD, LANES), lambda b, h, q: (b, 0)),
                  pl.BlockSpec((T_PAD, two), lambda b, h, q: (b, h))],
        out_specs=pl.BlockSpec((tq, two), lambda b, h, q: (b * nq + q, h)),
        out_shape=jax.ShapeDtypeStruct((ROWS_P, MLA_HEADS * V_DIM), BF),
        scratch_shapes=[pltpu.VMEM((2, nq, tq, tq), F32), pltpu.VMEM((2, tq, LANES), F32),
                        pltpu.VMEM((2, tq, LANES), F32), pltpu.VMEM((2, tq, V_DIM), F32)],
        compiler_params=_cp(("parallel", "parallel", "arbitrary")),
        name="mla_flash",
    )(qn, qp, kn, kk, v)


def _absorb_body(qn_ref, wuk_ref, o_ref):
    o_ref[...] = _dot_nt(qn_ref[...], wuk_ref[...]).astype(o_ref.dtype)


def _absorb(qn, wuk_b, layer):
    return pl.pallas_call(
        _absorb_body,
        grid=(MLA_HEADS,),
        in_specs=[pl.BlockSpec((ROWS_S, QK_NOPE), lambda h: (0, h)),
                  pl.BlockSpec((None, KV_LORA, QK_NOPE), lambda h: (layer, 0, h))],
        out_specs=pl.BlockSpec((ROWS_S, KV_LORA), lambda h: (0, h)),
        out_shape=jax.ShapeDtypeStruct((ROWS_S, MLA_HEADS * KV_LORA), BF),
        compiler_params=_cp(("parallel",)),
        name="mla_absorb",
    )(qn, wuk_b)


def _unabsorb_body(ol_ref, wuv_ref, o_ref):
    o_ref[...] = _dot(ol_ref[...], wuv_ref[...]).astype(o_ref.dtype)


def _unabsorb(ol, wuv_b, layer):
    return pl.pallas_call(
        _unabsorb_body,
        grid=(MLA_HEADS,),
        in_specs=[pl.BlockSpec((ROWS_S, KV_LORA), lambda h: (0, h)),
                  pl.BlockSpec((None, KV_LORA, V_DIM), lambda h: (layer, 0, h))],
        out_specs=pl.BlockSpec((ROWS_S, V_DIM), lambda h: (0, h)),
        out_shape=jax.ShapeDtypeStruct((ROWS_S, MLA_HEADS * V_DIM), BF),
        compiler_params=_cp(("parallel",)),
        name="mla_unabsorb",
    )(ol, wuv_b)


QROWS = DEC_SEQ * MLA_HEADS


def _paged_body(pt_ref, ql_ref, qp_ref, cn_ref, kn_ref, ckv_hbm, kpe_hbm, o_ref,
                cbuf, kbuf, sem, m_s, l_s, acc_s, newc_s, newk_s, *, layer, n_chunks, n_pages):
    b = pl.program_id(0)
    page = LANES

    def chunk_copies(seq, c, slot, known_pages):
        out = []
        for i in range(n_pages):
            pid = pt_ref[seq, c * n_pages + i] if known_pages else 0
            out.append(pltpu.make_async_copy(ckv_hbm.at[layer, pid],
                                             cbuf.at[slot, pl.ds(i * page, page), :], sem.at[slot]))
            out.append(pltpu.make_async_copy(kpe_hbm.at[layer, pid],
                                             kbuf.at[slot, :, pl.ds(i * page, page)], sem.at[slot]))
        return out

    @pl.when(b == 0)
    def _():
        for cp in chunk_copies(0, 0, 0, True):
            cp.start()

    m_s[...] = jnp.full_like(m_s, NEG)
    l_s[...] = jnp.zeros_like(l_s)
    acc_s[...] = jnp.zeros_like(acc_s)

    ql = ql_ref[...].reshape(QROWS, KV_LORA)
    qp = qp_ref[...].reshape(QROWS, QK_ROPE)

    def local_softmax(s, vb):
        m = jnp.max(s, axis=1, keepdims=True)
        p = jnp.exp(s - m)
        return m, jnp.sum(p, axis=1, keepdims=True), _dot(p.astype(BF), vb)

    def merge(parts):
        m_prev = m_s[...]
        m_new = m_prev
        for m, _, _ in parts:
            m_new = jnp.maximum(m_new, m)
        alpha = jnp.exp(m_prev - m_new)
        l = alpha * l_s[...]
        acc = alpha * acc_s[...]
        for m, lg, ag in parts:
            w = jnp.exp(m - m_new)
            l = l + w * lg
            acc = acc + w * ag
        m_s[...] = m_new
        l_s[...] = l
        acc_s[...] = acc

    for c in range(n_chunks):
        slot = c % 2
        if c + 1 < n_chunks:
            for cp in chunk_copies(b, c + 1, 1 - slot, True):
                cp.start()
        else:
            @pl.when(b + 1 < pl.num_programs(0))
            def _():
                for cp in chunk_copies(b + 1, 0, 1 - slot, True):
                    cp.start()
        for cp in chunk_copies(b, c, slot, False):
            cp.wait()
        cb = cbuf[slot].astype(BF)
        kt = kbuf[slot].astype(BF)
        merge([local_softmax((_dot_nt(ql, cb) + _dot(qp, kt)) * MLA_SCALE, cb)])

    newc_s[...] = jnp.zeros_like(newc_s)
    newk_s[...] = jnp.zeros_like(newk_s)
    newc_s[0:DEC_SEQ, :] = cn_ref[...]
    newk_s[0:DEC_SEQ, :] = kn_ref[...]
    cn = newc_s[...].astype(BF)
    s = (_dot_nt(ql, cn) + _dot_nt(qp, newk_s[...].astype(BF))) * MLA_SCALE
    qt = lax.broadcasted_iota(jnp.int32, (QROWS, LANES), 0) // MLA_HEADS
    kt = lax.broadcasted_iota(jnp.int32, (QROWS, LANES), 1)
    merge([local_softmax(jnp.where(kt <= qt, s, NEG), cn)])
    o = acc_s[...] / l_s[...]
    o_ref[...] = o.reshape(DEC_SEQ, MLA_HEADS, KV_LORA).astype(o_ref.dtype)


def _paged_attention(page_table, q_lat, q_pe, ckv_new, kpe_new, cache_ckv, cache_kpe_t, layer):
    n_pages = PAGES_PER_STEP
    n_chunks = page_table.shape[1] // n_pages
    assert n_chunks % 2 == 0 and cache_ckv.shape[2] == LANES
    keys = n_pages * LANES
    body = functools.partial(_paged_body, layer=layer, n_chunks=n_chunks, n_pages=n_pages)
    qspec = lambda w: pl.BlockSpec((DEC_SEQ, None, MLA_HEADS, w), lambda b, pt: (0, b, 0, 0))
    nspec = lambda w: pl.BlockSpec((None, DEC_SEQ, w), lambda b, pt: (b, 0, 0))
    grid_spec = pltpu.PrefetchScalarGridSpec(
        num_scalar_prefetch=1,
        grid=(DEC_BATCH,),
        in_specs=[qspec(KV_LORA), qspec(QK_ROPE), nspec(KV_LORA), nspec(QK_ROPE),
                  pl.BlockSpec(memory_space=pl.ANY), pl.BlockSpec(memory_space=pl.ANY)],
        out_specs=pl.BlockSpec((DEC_SEQ, None, MLA_HEADS, KV_LORA), lambda b, pt: (0, b, 0, 0)),
        scratch_shapes=[pltpu.VMEM((2, keys, KV_LORA), F32), pltpu.VMEM((2, QK_ROPE, keys), F32),
                        pltpu.SemaphoreType.DMA((2,)),
                        pltpu.VMEM((QROWS, 1), F32), pltpu.VMEM((QROWS, 1), F32),
                        pltpu.VMEM((QROWS, KV_LORA), F32),
                        pltpu.VMEM((LANES, KV_LORA), F32), pltpu.VMEM((LANES, QK_ROPE), F32)],
    )
    return pl.pallas_call(
        body,
        grid_spec=grid_spec,
        out_shape=jax.ShapeDtypeStruct((DEC_SEQ, DEC_BATCH, MLA_HEADS, KV_LORA), BF),
        compiler_params=_cp(("arbitrary",)),
        name="mla_paged",
    )(page_table, q_lat, q_pe, ckv_new, kpe_new, cache_ckv, cache_kpe_t)


def _rope_tables(pos):
    half = QK_ROPE // 2
    inv = ROPE_THETA ** (-jnp.arange(half, dtype=F32) / half)
    ang = pos.astype(F32)[:, None] * inv[None, :]
    cos, sin = jnp.cos(ang), jnp.sin(ang)
    return (jnp.concatenate([cos, cos, cos, cos], axis=1),
            jnp.concatenate([-sin, sin, -sin, sin], axis=1))


def _swap_halves(w):
    half = w.shape[-1] // 2
    return jnp.concatenate([w[..., half:], w[..., :half]], axis=-1)


def kernel(x_prompt, x_sample, cache_mla_ckv, cache_mla_kpe, page_table, state_s5_re, state_s5_im, state_mlstm_c, state_mlstm_n, state_mlstm_m, state_ffn_conv, meta_tokens, ln1_g, ln1_b, ln2_g, ln2_b, mix_w_in, mix_b_gates, s5_a_re, s5_a_im, s5_log_dt, s5_b_re, s5_b_im, s5_c_re, s5_c_im, s5_d, s5_w_glu, s5_b_glu, ml_norm_g, mix_w_out, mla_w_in, mla_q_norm_g, mla_kv_norm_g, mla_w_uq, mla_w_uk, mla_w_uv, mla_w_out, ffn_w_up, ffn_conv_w, ffn_conv_b, ffn_w_down):
    n_mix = mix_w_in.shape[0]
    n_mla = mla_w_in.shape[0]
    past = page_table.shape[1] * cache_mla_ckv.shape[2]

    n_main = S5_WIDTH + 4 * ML_WIDTH
    mix_w_in_b = mix_w_in.astype(BF)
    mix_w_g_b = jnp.pad(mix_w_in[:, :, n_main:], ((0, 0), (0, 0), (0, LANES - 2 * ML_HEADS))).astype(BF)
    gate_bias = jnp.pad(mix_b_gates, ((0, 0), (0, LANES - 2 * ML_HEADS)))
    w_glu_b = s5_w_glu.astype(BF)
    mix_w_out_b = mix_w_out.astype(BF)
    o = Q_LORA + KV_LORA
    w_kpe = mla_w_in[:, :, o:]
    w_kpe_sw = _swap_halves(w_kpe)
    mla_w_in_b = jnp.concatenate([mla_w_in[:, :, :o], w_kpe, w_kpe, w_kpe_sw, w_kpe_sw], axis=-1).astype(BF)
    wq4 = mla_w_uq.reshape(n_mla, Q_LORA, MLA_HEADS, QK_NOPE + QK_ROPE)
    wq_pe = wq4[..., QK_NOPE:]
    mla_w_uq_b = jnp.concatenate(
        [wq4[..., :QK_NOPE].reshape(n_mla, Q_LORA, -1), wq_pe.reshape(n_mla, Q_LORA, -1),
         _swap_halves(wq_pe).reshape(n_mla, Q_LORA, -1)], axis=-1).astype(BF)
    mla_w_uk_b = mla_w_uk.reshape(n_mla, KV_LORA, MLA_HEADS * QK_NOPE).astype(BF)
    mla_w_uv_b = mla_w_uv.reshape(n_mla, KV_LORA, MLA_HEADS * V_DIM).astype(BF)
    mla_w_out_b = mla_w_out.astype(BF)
    ffn_w_up_b = ffn_w_up.astype(BF)
    ffn_w_down_b = ffn_w_down.astype(BF)
    conv_b3 = ffn_conv_b[:, None, :]
    prev_conv = state_ffn_conv.reshape(DEPTH, DEC_BATCH, 4 * D_FF)
    cache_kpe_t = jnp.swapaxes(cache_mla_kpe, 2, 3)
    cos_p, sin_p = _rope_tables(jnp.arange(T_PAD))
    cos_p = jnp.concatenate([cos_p] * BATCH, axis=0)
    sin_p = jnp.concatenate([sin_p] * BATCH, axis=0)
    cos_s, sin_s = _rope_tables(past + jnp.arange(DEC_SEQ))
    cos_s = jnp.repeat(cos_s, DEC_BATCH, axis=0)
    sin_s = jnp.repeat(sin_s, DEC_BATCH, axis=0)

    xp = jnp.concatenate([jnp.broadcast_to(meta_tokens[None], (BATCH, N_META, D_MODEL)), x_prompt], axis=1)
    xp = jnp.pad(xp, ((0, 0), (0, T_PAD - T_REAL), (0, 0))).reshape(ROWS_P, D_MODEL)
    xs = x_sample.transpose(1, 0, 2).reshape(ROWS_S, D_MODEL)
    tm_p, tm_s = 704, ROWS_S

    outs = {k: [] for k in ("ckv_p", "kpe_p", "ckv_s", "kpe_s", "s5r_p", "s5i_p", "s5r_s", "s5i_s",
                            "mc_p", "mn_p", "mm_p", "mc_s", "mn_s", "mm_s", "conv_p", "conv_s")}

    for l in range(DEPTH):
        j = l // 2
        g1, b1 = ln1_g[l][None, :], ln1_b[l][None, :]
        g2, b2 = ln2_g[l][None, :], ln2_b[l][None, :]
        if l % 2 == 0:
            ab_re, ab_im, bb_re, bb_im = _s5_params(s5_a_re[j], s5_a_im[j], s5_log_dt[j], s5_b_re[j], s5_b_im[j])
            bre, bim = _blockdiag_in(bb_re), _blockdiag_in(bb_im)
            cre, cim = _blockdiag_out(s5_c_re[j]), _blockdiag_out(s5_c_im[j])
            d_row = s5_d[j].reshape(1, S5_WIDTH)
            bias_row = gate_bias[j][None, :]
            ng_row = ml_norm_g[j][None, :]
            bglu_row = s5_b_glu[j][None, :]

            proj, gates = _mix_proj(xp, mix_w_in_b, mix_w_g_b, j, tm_p)
            y5, hr, hi = _s5_prompt(proj, bre, bim, cre, cim, d_row, ab_re, ab_im)
            y5 = _glu(y5, w_glu_b, bglu_row, j, tm_p)
            yml, c1, n1, m1 = _mlstm_prompt(proj, gates, bias_row, ng_row)
            xp = _proj_ln([y5, yml], mix_w_out_b, j, xp, g1, b1, 352)
            outs["s5r_p"].append(hr); outs["s5i_p"].append(hi)
            outs["mc_p"].append(c1); outs["mn_p"].append(n1); outs["mm_p"].append(m1)

            proj, gates = _mix_proj(xs, mix_w_in_b, mix_w_g_b, j, tm_s)
            y5, hr, hi = _s5_sample(proj, state_s5_re[j], state_s5_im[j], bre, bim, cre, cim, d_row, ab_re, ab_im)
            y5 = _glu(y5, w_glu_b, bglu_row, j, tm_s)
            proj_bm = proj.reshape(DEC_SEQ, DEC_BATCH, n_main).transpose(1, 0, 2)
            gates_bm = gates.reshape(DEC_SEQ, DEC_BATCH, LANES).transpose(1, 0, 2)
            yml, c1, n1, m1 = _mlstm_sample(proj_bm, gates_bm, bias_row, ng_row,
                                            state_mlstm_c, state_mlstm_n, state_mlstm_m, j)
            yml = yml.transpose(1, 0, 2).reshape(ROWS_S, ML_WIDTH)
            xs = _proj_ln([y5, yml], mix_w_out_b, j, xs, g1, b1, 256)
            outs["s5r_s"].append(hr); outs["s5i_s"].append(hi)
            outs["mc_s"].append(c1); outs["mn_s"].append(n1); outs["mm_s"].append(m1)
        else:
            gq_row = mla_q_norm_g[j][None, :]
            gkv_row = mla_kv_norm_g[j][None, :]

            ckv, kk, qn, qp = _mla_in(xp, mla_w_in_b, gq_row, gkv_row, mla_w_uq_b, cos_p, sin_p, j, 352)
            kn = _mm(ckv, mla_w_uk_b, j, BF, 384, MLA_HEADS * QK_NOPE)
            vv = _mm(ckv, mla_w_uv_b, j, BF, 384, MLA_HEADS * V_DIM)
            att = _flash(qn, qp, kn, kk, vv)
            xp = _proj_ln([att], mla_w_out_b, j, xp, g1, b1, 352)
            outs["ckv_p"].append(ckv.reshape(BATCH, T_PAD, KV_LORA)[:, :T_REAL])
            outs["kpe_p"].append(kk.reshape(BATCH, T_PAD, LANES)[:, :T_REAL, :QK_ROPE])

            ckv, kk, qn, qp = _mla_in(xs, mla_w_in_b, gq_row, gkv_row, mla_w_uq_b, cos_s, sin_s, j, 256)
            q_lat = _absorb(qn, mla_w_uk_b, j)
            ckv_bm = ckv.reshape(DEC_SEQ, DEC_BATCH, KV_LORA).transpose(1, 0, 2)
            kpe_bm = kk[:, :QK_ROPE].reshape(DEC_SEQ, DEC_BATCH, QK_ROPE).transpose(1, 0, 2)
            o_lat = _paged_attention(
                page_table,
                q_lat.reshape(DEC_SEQ, DEC_BATCH, MLA_HEADS, KV_LORA),
                qp.reshape(DEC_SEQ, DEC_BATCH, MLA_HEADS, QK_ROPE),
                ckv_bm, kpe_bm, cache_mla_ckv, cache_kpe_t, j)
            att = _unabsorb(o_lat.reshape(ROWS_S, MLA_HEADS * KV_LORA), mla_w_uv_b, j)
            xs = _proj_ln([att], mla_w_out_b, j, xs, g1, b1, 256)
            outs["ckv_s"].append(ckv_bm)
            outs["kpe_s"].append(kpe_bm)

        tail = jnp.concatenate([xp[b * T_PAD + T_REAL - 2:b * T_PAD + T_REAL] for b in range(BATCH)]
                               + [jnp.zeros((EXTRA - 2 * BATCH, D_MODEL), F32)], axis=0)
        xs_ext = jnp.concatenate([xs, tail], axis=0)
        xs, cs, cpv = _ffn_sample(xs_ext, prev_conv, ffn_w_up_b, ffn_conv_w, conv_b3, ffn_w_down_b, l, g2, b2)
        xp = _ffn_prompt(xp, ffn_w_up_b, ffn_conv_w, conv_b3, ffn_w_down_b, l, g2, b2)
        outs["conv_s"].append(cs.transpose(1, 0, 2).reshape(DEC_BATCH, 2, 2 * D_FF))
        cp2 = jnp.concatenate([cpv[0, :2 * BATCH], cpv[1, :2 * BATCH]], axis=-1)
        outs["conv_p"].append(cp2.reshape(BATCH, 2, 2 * D_FF))

    st = lambda k: jnp.stack(outs[k])
    y_prompt = xp.reshape(BATCH, T_PAD, D_MODEL)[:, N_META:T_REAL]
    y_sample = xs.reshape(DEC_SEQ, DEC_BATCH, D_MODEL).transpose(1, 0, 2)
    return (y_prompt, y_sample,
            st("ckv_p"), st("kpe_p"), st("ckv_s"), st("kpe_s"),
            st("s5r_p"), st("s5i_p"), st("s5r_s"), st("s5i_s"),
            st("mc_p"), st("mn_p"), st("mm_p"), st("mc_s"), st("mn_s"), st("mm_s"),
            st("conv_p"), st("conv_s"))
```

```python
import functools
import math

import jax
import jax.numpy as jnp
from jax import lax
from jax.experimental import pallas as pl
from jax.experimental.pallas import tpu as pltpu

BF = jnp.bfloat16
F32 = jnp.float32

D_MODEL = 2048
BATCH = 2
SEQ = 4096
DEPTH = 4
DEC_BATCH = 128
DEC_SEQ = 4
N_META = 16
S5_WIDTH = 1024
S5_GROUP = 16
S5_GROUPS = 64
S5_STATE = 64
ML_WIDTH = 1024
ML_HEADS = 8
ML_HEAD_DIM = 128
MLA_HEADS = 16
Q_LORA = 512
KV_LORA = 512
QK_NOPE = 128
QK_ROPE = 64
V_DIM = 128
ROPE_THETA = 10000.0
MLA_SCALE = (QK_NOPE + QK_ROPE) ** -0.5
D_FF = 5632
ALPHA = (2 * DEPTH) ** 0.25
LN_EPS = 1e-5
RMS_EPS = 1e-6
NEG = -1e30
LOG2E = 1.4426950408889634

LANES = 128
SUBLANES = 8
BF16_ROWS = 16
VMEM_LIMIT = 56 * 1024 * 1024

T_REAL = N_META + SEQ
T_PAD = 4224
ROWS_P = BATCH * T_PAD
ROWS_S = DEC_BATCH * DEC_SEQ
S5_BLK = 512
N_S5_BLK = S5_GROUPS * S5_STATE // S5_BLK
PAGES_PER_STEP = 16


def _cp(sem, vmem=VMEM_LIMIT):
    return pltpu.CompilerParams(dimension_semantics=sem, vmem_limit_bytes=vmem)


def _dot(a, b):
    return jnp.dot(a, b, preferred_element_type=F32)


def _dot_nt(a, b):
    return lax.dot_general(a, b, (((1,), (1,)), ((), ())), preferred_element_type=F32)


def _sigmoid(x):
    return 1.0 / (1.0 + jnp.exp(-x))


def _layer_norm(xf, g, b):
    mu = jnp.mean(xf, axis=-1, keepdims=True)
    xc = xf - mu
    var = jnp.mean(xc * xc, axis=-1, keepdims=True)
    return xc * lax.rsqrt(var + LN_EPS) * g + b


def _mm_body(x_ref, w_ref, o_ref):
    o_ref[...] = _dot(x_ref[...].astype(BF), w_ref[...]).astype(o_ref.dtype)


def _mm(x, w, layer, out_dtype, tm, tn):
    rows, k = x.shape
    n = w.shape[-1]
    return pl.pallas_call(
        _mm_body,
        grid=(rows // tm, n // tn),
        in_specs=[pl.BlockSpec((tm, k), lambda i, j: (i, 0)),
                  pl.BlockSpec((None, k, tn), lambda i, j: (layer, 0, j))],
        out_specs=pl.BlockSpec((tm, tn), lambda i, j: (i, j)),
        out_shape=jax.ShapeDtypeStruct((rows, n), out_dtype),
        compiler_params=_cp(("parallel", "arbitrary")),
        name="mm",
    )(x, w)


def _mix_proj_body(x_ref, w_ref, wg_ref, p_ref, g_ref, xb_s):
    @pl.when(pl.program_id(1) == 0)
    def _():
        xb = x_ref[...].astype(BF)
        xb_s[...] = xb
        g_ref[...] = _dot(xb, wg_ref[...])

    p_ref[...] = _dot(xb_s[...], w_ref[...])


def _mix_proj(x, w_in_b, w_g_b, layer, tm):
    rows = x.shape[0]
    tn = 512
    n_main = S5_WIDTH + 4 * ML_WIDTH
    return pl.pallas_call(
        _mix_proj_body,
        grid=(rows // tm, n_main // tn),
        in_specs=[pl.BlockSpec((tm, D_MODEL), lambda i, j: (i, 0)),
                  pl.BlockSpec((None, D_MODEL, tn), lambda i, j: (layer, 0, j)),
                  pl.BlockSpec((None, D_MODEL, LANES), lambda i, j: (layer, 0, 0))],
        out_specs=[pl.BlockSpec((tm, tn), lambda i, j: (i, j)),
                   pl.BlockSpec((tm, LANES), lambda i, j: (i, 0))],
        out_shape=[jax.ShapeDtypeStruct((rows, n_main), F32),
                   jax.ShapeDtypeStruct((rows, LANES), F32)],
        scratch_shapes=[pltpu.VMEM((tm, D_MODEL), BF)],
        compiler_params=_cp(("parallel", "arbitrary")),
        name="mix_proj",
    )(x, w_in_b, w_g_b)


def _s5_params_body(are_ref, aim_ref, ldt_ref, bre_ref, bim_ref, abr_ref, abi_ref, bbr_ref, bbi_ref):
    lam_re = are_ref[...]
    lam_im = aim_ref[...]
    dt = jnp.exp(ldt_ref[...])
    mag = jnp.exp(lam_re * dt)
    ab_re = mag * jnp.cos(lam_im * dt)
    ab_im = mag * jnp.sin(lam_im * dt)
    den = lam_re * lam_re + lam_im * lam_im
    z_re = ((ab_re - 1.0) * lam_re + ab_im * lam_im) / den
    z_im = (ab_im * lam_re - (ab_re - 1.0) * lam_im) / den
    br = bre_ref[...]
    bi = bim_ref[...]
    abr_ref[...] = ab_re
    abi_ref[...] = ab_im
    bbr_ref[...] = z_re * br - z_im * bi
    bbi_ref[...] = z_re * bi + z_im * br


def _s5_params(a_re, a_im, log_dt, b_re, b_im):
    n = S5_GROUPS * S5_STATE
    col = lambda a: a.reshape(n, 1)
    ldt = jnp.broadcast_to(log_dt[:, None], (S5_GROUPS, S5_STATE)).reshape(n, 1)
    out = pl.pallas_call(
        _s5_params_body,
        out_shape=[jax.ShapeDtypeStruct((n, 1), F32), jax.ShapeDtypeStruct((n, 1), F32),
                   jax.ShapeDtypeStruct((n, S5_GROUP), F32), jax.ShapeDtypeStruct((n, S5_GROUP), F32)],
        name="s5_params",
    )(col(a_re), col(a_im), ldt, b_re.reshape(n, S5_GROUP), b_im.reshape(n, S5_GROUP))
    ab_re, ab_im, bb_re, bb_im = out
    return ab_re.reshape(1, n), ab_im.reshape(1, n), bb_re, bb_im


def _blockdiag_in(bb):
    nb = N_S5_BLK
    g = S5_GROUPS // nb
    b4 = bb.reshape(nb, g, S5_STATE, S5_GROUP).transpose(0, 1, 3, 2)
    eye = jnp.eye(g, dtype=bool)[None, :, None, :, None]
    out = jnp.where(eye, b4[:, :, :, None, :], 0.0)
    return out.reshape(nb, g * S5_GROUP, g * S5_STATE).astype(BF)


def _blockdiag_out(c):
    nb = N_S5_BLK
    g = S5_GROUPS // nb
    c4 = c.reshape(nb, g, S5_GROUP, S5_STATE).transpose(0, 1, 3, 2)
    eye = jnp.eye(g, dtype=bool)[None, :, None, :, None]
    out = jnp.where(eye, c4[:, :, :, None, :], 0.0)
    return out.reshape(nb, g * S5_STATE, g * S5_GROUP).astype(BF)


def _cmul(ar, ai, br, bi):
    return ar * br - ai * bi, ar * bi + ai * br


def _s5_prompt_body(u_ref, bre_ref, bim_ref, cre_ref, cim_ref, d_ref, ar_ref, ai_ref,
                    y_ref, hre_ref, him_ref, xr_s, xi_s, car_s, cai_s, *, tc, t_final):
    c = pl.program_id(2)

    @pl.when(c == 0)
    def _():
        car_s[...] = jnp.zeros_like(car_s)
        cai_s[...] = jnp.zeros_like(cai_s)

    u = u_ref[...]
    ub = u.astype(BF)
    xr_s[...] = _dot(ub, bre_ref[...])
    xi_s[...] = _dot(ub, bim_ref[...])

    a1 = (ar_ref[...], ai_ref[...])
    a2 = _cmul(*a1, *a1)
    a3 = _cmul(*a2, *a1)
    a4 = _cmul(*a2, *a2)
    a5 = _cmul(*a4, *a1)
    a6 = _cmul(*a4, *a2)
    a7 = _cmul(*a4, *a3)
    a8 = _cmul(*a4, *a4)
    pows = (a1, a2, a3, a4, a5, a6, a7, a8)
    pw_r = jnp.concatenate([p[0] for p in pows], axis=0)
    pw_i = jnp.concatenate([p[1] for p in pows], axis=0)
    sub = lax.broadcasted_iota(jnp.int32, (SUBLANES, S5_BLK), 0)
    steps = [(d, jnp.where(sub >= d, pr, 0.0), jnp.where(sub >= d, pi, 0.0))
             for d, (pr, pi) in ((1, a1), (2, a2), (4, a4))]

    def tile(n, carry):
        cr, ci = carry
        off = pl.multiple_of(n * SUBLANES, SUBLANES)
        xr = xr_s[pl.ds(off, SUBLANES), :]
        xi = xi_s[pl.ds(off, SUBLANES), :]
        for d, pr, pi in steps:
            sr = pltpu.roll(xr, d, 0)
            si = pltpu.roll(xi, d, 0)
            xr, xi = xr + pr * sr - pi * si, xi + pr * si + pi * sr
        hr = xr + pw_r * cr - pw_i * ci
        hi = xi + pw_r * ci + pw_i * cr
        xr_s[pl.ds(off, SUBLANES), :] = hr
        xi_s[pl.ds(off, SUBLANES), :] = hi
        return hr[SUBLANES - 1:SUBLANES, :], hi[SUBLANES - 1:SUBLANES, :]

    cr, ci = lax.fori_loop(0, tc // SUBLANES, tile, (car_s[...], cai_s[...]), unroll=4)
    car_s[...] = cr
    cai_s[...] = ci

    hr = xr_s[...]
    hi = xi_s[...]
    y_ref[...] = _dot(hr.astype(BF), cre_ref[...]) - _dot(hi.astype(BF), cim_ref[...]) + d_ref[...] * u

    @pl.when(c == t_final // tc)
    def _():
        r = t_final % tc
        hre_ref[...] = jnp.broadcast_to(xr_s[r:r + 1, :], hre_ref.shape)
        him_ref[...] = jnp.broadcast_to(xi_s[r:r + 1, :], him_ref.shape)


def _s5_prompt(proj, bre, bim, cre, cim, d_row, ab_re, ab_im):
    tc = 384
    nt = T_PAD // tc
    body = functools.partial(_s5_prompt_body, tc=tc, t_final=T_REAL - 1)
    wspec_in = pl.BlockSpec((None, LANES, S5_BLK), lambda b, s, c: (s, 0, 0))
    wspec_out = pl.BlockSpec((None, S5_BLK, LANES), lambda b, s, c: (s, 0, 0))
    aspec = pl.BlockSpec((1, S5_BLK), lambda b, s, c: (0, s))
    st_spec = pl.BlockSpec((None, None, SUBLANES, S5_BLK), lambda b, s, c: (b, s, 0, 0))
    st_shape = jax.ShapeDtypeStruct((BATCH, N_S5_BLK, SUBLANES, S5_BLK), F32)
    y, hre, him = pl.pallas_call(
        body,
        grid=(BATCH, N_S5_BLK, nt),
        in_specs=[pl.BlockSpec((tc, LANES), lambda b, s, c: (b * nt + c, s)),
                  wspec_in, wspec_in, wspec_out, wspec_out,
                  pl.BlockSpec((1, LANES), lambda b, s, c: (0, s)),
                  aspec, aspec],
        out_specs=[pl.BlockSpec((tc, LANES), lambda b, s, c: (b * nt + c, s)), st_spec, st_spec],
        out_shape=[jax.ShapeDtypeStruct((ROWS_P, S5_WIDTH), F32), st_shape, st_shape],
        scratch_shapes=[pltpu.VMEM((tc, S5_BLK), F32), pltpu.VMEM((tc, S5_BLK), F32),
                        pltpu.VMEM((1, S5_BLK), F32), pltpu.VMEM((1, S5_BLK), F32)],
        compiler_params=_cp(("parallel", "parallel", "arbitrary")),
        name="s5_prompt",
    )(proj, bre, bim, cre, cim, d_row, ab_re, ab_im)
    fin = lambda h: h[:, :, 0, :].reshape(BATCH, S5_GROUPS, S5_STATE)
    return y, fin(hre), fin(him)


def _s5_sample_body(u_ref, h0r_ref, h0i_ref, bre_ref, bim_ref, cre_ref, cim_ref, d_ref, ar_ref, ai_ref,
                    y_ref, hre_ref, him_ref):
    u = u_ref[...]
    ub = u.astype(BF)
    bur = _dot(ub, bre_ref[...])
    bui = _dot(ub, bim_ref[...])
    ar = ar_ref[...]
    ai = ai_ref[...]
    hr = h0r_ref[...]
    hi = h0i_ref[...]
    hrs, his = [], []
    for t in range(DEC_SEQ):
        lo = t * DEC_BATCH
        hr, hi = (ar * hr - ai * hi + bur[lo:lo + DEC_BATCH], ar * hi + ai * hr + bui[lo:lo + DEC_BATCH])
        hrs.append(hr)
        his.append(hi)
    hra = jnp.concatenate(hrs, axis=0)
    hia = jnp.concatenate(his, axis=0)
    y_ref[...] = _dot(hra.astype(BF), cre_ref[...]) - _dot(hia.astype(BF), cim_ref[...]) + d_ref[...] * u
    hre_ref[...] = hr
    him_ref[...] = hi


def _s5_sample(proj, h0_re, h0_im, bre, bim, cre, cim, d_row, ab_re, ab_im):
    n = S5_GROUPS * S5_STATE
    wspec_in = pl.BlockSpec((None, LANES, S5_BLK), lambda s: (s, 0, 0))
    wspec_out = pl.BlockSpec((None, S5_BLK, LANES), lambda s: (s, 0, 0))
    aspec = pl.BlockSpec((1, S5_BLK), lambda s: (0, s))
    hspec = pl.BlockSpec((DEC_BATCH, S5_BLK), lambda s: (0, s))
    y, hre, him = pl.pallas_call(
        _s5_sample_body,
        grid=(N_S5_BLK,),
        in_specs=[pl.BlockSpec((ROWS_S, LANES), lambda s: (0, s)), hspec, hspec,
                  wspec_in, wspec_in, wspec_out, wspec_out,
                  pl.BlockSpec((1, LANES), lambda s: (0, s)), aspec, aspec],
        out_specs=[pl.BlockSpec((ROWS_S, LANES), lambda s: (0, s)), hspec, hspec],
        out_shape=[jax.ShapeDtypeStruct((ROWS_S, S5_WIDTH), F32),
                   jax.ShapeDtypeStruct((DEC_BATCH, n), F32), jax.ShapeDtypeStruct((DEC_BATCH, n), F32)],
        compiler_params=_cp(("parallel",)),
        name="s5_sample",
    )(proj, h0_re.reshape(DEC_BATCH, n), h0_im.reshape(DEC_BATCH, n), bre, bim, cre, cim, d_row, ab_re, ab_im)
    return y, hre.reshape(DEC_BATCH, S5_GROUPS, S5_STATE), him.reshape(DEC_BATCH, S5_GROUPS, S5_STATE)


def _glu_body(y_ref, w_ref, b_ref, o_ref):
    y = y_ref[...]
    z = 0.5 * y * (1.0 + jnp.tanh(math.sqrt(2.0 / math.pi) * (y + 0.044715 * (y * y * y))))
    o_ref[...] = (z * _sigmoid(_dot(z.astype(BF), w_ref[...]) + b_ref[...])).astype(o_ref.dtype)


def _glu(y, w_b, b_row, layer, tm):
    rows = y.shape[0]
    return pl.pallas_call(
        _glu_body,
        grid=(rows // tm,),
        in_specs=[pl.BlockSpec((tm, S5_WIDTH), lambda i: (i, 0)),
                  pl.BlockSpec((None, S5_WIDTH, S5_WIDTH), lambda i: (layer, 0, 0)),
                  pl.BlockSpec((1, S5_WIDTH), lambda i: (0, 0))],
        out_specs=pl.BlockSpec((tm, S5_WIDTH), lambda i: (i, 0)),
        out_shape=jax.ShapeDtypeStruct((rows, S5_WIDTH), BF),
        compiler_params=_cp(("parallel",)),
        name="s5_glu",
    )(y, w_b, b_row)


def _log_sigmoid(x):
    return jnp.minimum(x, 0.0) - jnp.log1p(jnp.exp(-jnp.abs(x)))


def _split_dot(tri_b, x):
    hi = x.astype(BF)
    r1 = x - hi.astype(F32)
    mid = r1.astype(BF)
    lo = (r1 - mid.astype(F32)).astype(BF)
    return _dot(tri_b, hi) + _dot(tri_b, mid) + _dot(tri_b, lo)


def _mlstm_body(*refs, chunk, n_valid, rows_in, has_init):
    q_ref, k_ref, v_ref, og_ref, g_ref, bias_ref, ng_ref = refs[:7]
    pos = 7
    if has_init:
        c0_ref, n0_ref, m0_ref = refs[pos:pos + 3]
        pos += 3
    y_ref, c_ref, n_ref, m_ref = refs[pos:pos + 4]
    pads = refs[pos + 4:]
    c = pl.program_id(1)
    L = chunk

    @pl.when(c == 0)
    def _():
        if has_init:
            c_ref[...] = c0_ref[...]
            n_ref[...] = n0_ref[...]
            m_ref[...] = m0_ref[...]
        else:
            c_ref[...] = jnp.zeros_like(c_ref)
            n_ref[...] = jnp.zeros_like(n_ref)
            m_ref[...] = jnp.zeros_like(m_ref)

    if rows_in < L:
        @pl.when(pl.program_id(0) == 0)
        def _():
            for p in pads:
                p[...] = jnp.zeros_like(p)

        for src, p in zip((q_ref, k_ref, v_ref, og_ref, g_ref), pads):
            p[0:rows_in, :] = src[...]
        q_src, k_src, v_src, og_src, g_src = pads
    else:
        q_src, k_src, v_src, og_src, g_src = q_ref, k_ref, v_ref, og_ref, g_ref

    rid = lax.broadcasted_iota(jnp.int32, (L, 1), 0) + c * L
    valid = rid < n_valid
    lane = lax.broadcasted_iota(jnp.int32, (1, LANES), 1)
    g = g_src[...] + bias_ref[...]
    lf = jnp.where(valid, _log_sigmoid(g), 0.0)
    x = jnp.where(lane < ML_HEADS, jnp.where(valid, g, NEG), lf)
    ti = lax.broadcasted_iota(jnp.int32, (L, L), 0)
    si = lax.broadcasted_iota(jnp.int32, (L, L), 1)
    causal = si <= ti
    tri_b = jnp.where(causal, 1.0, 0.0).astype(BF)
    bc = _split_dot(tri_b, lf)
    xt = x.T
    bt = bc.T
    scale = ML_HEAD_DIM ** -0.5

    for h in range(ML_HEADS):
        sl = slice(h * ML_HEAD_DIM, (h + 1) * ML_HEAD_DIM)
        q = q_src[:, sl]
        k = k_src[:, sl] * scale
        v = v_src[:, sl]
        ig_col = x[:, h:h + 1]
        ig_row = xt[h:h + 1, :]
        b_col = bc[:, ML_HEADS + h:ML_HEADS + h + 1]
        b_row = bt[ML_HEADS + h:ML_HEADS + h + 1, :]
        m_prev = m_ref[h:h + 1, 0:1]
        cmat = c_ref[h]
        n_row = n_ref[h:h + 1, :]

        log_d = jnp.where(causal, b_col - b_row + ig_row, NEG)
        m_inter = b_col + m_prev
        m_t = jnp.maximum(m_inter, jnp.max(log_d, axis=1, keepdims=True))
        dmat = jnp.exp(log_d - m_t)
        w_inter = jnp.exp(m_inter - m_t)
        qb = q.astype(BF)
        kb = k.astype(BF)
        vb = v.astype(BF)
        s = _dot_nt(qb, kb) * dmat
        num = w_inter * _dot_nt(qb, cmat.astype(BF)) + _dot(s.astype(BF), vb)
        den = w_inter * jnp.sum(q * n_row, axis=1, keepdims=True) + jnp.sum(s, axis=1, keepdims=True)
        hh = num / jnp.maximum(jnp.abs(den), jnp.exp(-m_t))
        mu = jnp.mean(hh, axis=1, keepdims=True)
        hc = hh - mu
        var = jnp.mean(hc * hc, axis=1, keepdims=True)
        hn = hc * lax.rsqrt(var + LN_EPS) * ng_ref[:, sl]
        yv = (_sigmoid(og_src[:, sl]) * hn).astype(y_ref.dtype)
        if rows_in < L:
            y_ref[:, sl] = yv[0:rows_in, :]
        else:
            y_ref[:, sl] = yv

        m_new = m_t[L - 1:L, :]
        w_s = jnp.exp(b_col[L - 1:L, :] - b_col + ig_col - m_new)
        f_s = jnp.exp(m_inter[L - 1:L, :] - m_new)
        vw_t = (v * w_s).T
        c_ref[h] = f_s * cmat + _dot(vw_t.astype(BF), kb)
        n_ref[h:h + 1, :] = f_s * n_row + jnp.sum(k * w_s, axis=0, keepdims=True)
        m_ref[h:h + 1, :] = jnp.broadcast_to(m_new, (1, LANES))


def _mlstm_prompt(proj, gates, bias_row, ng_row):
    L = 128
    nc = T_PAD // L
    body = functools.partial(_mlstm_body, chunk=L, n_valid=T_REAL, rows_in=L, has_init=False)
    col = lambda off: pl.BlockSpec((L, ML_WIDTH), lambda b, c: (b * nc + c, off))
    st = lambda *tail: pl.BlockSpec((None,) + tail, lambda b, c: (b,) + (0,) * len(tail))
    y, cst, nst, mst = pl.pallas_call(
        body,
        grid=(BATCH, nc),
        in_specs=[col(1), col(2), col(3), col(4),
                  pl.BlockSpec((L, LANES), lambda b, c: (b * nc + c, 0)),
                  pl.BlockSpec((1, LANES), lambda b, c: (0, 0)),
                  pl.BlockSpec((1, ML_WIDTH), lambda b, c: (0, 0))],
        out_specs=[pl.BlockSpec((L, ML_WIDTH), lambda b, c: (b * nc + c, 0)),
                   st(ML_HEADS, ML_HEAD_DIM, ML_HEAD_DIM), st(ML_HEADS, ML_HEAD_DIM), st(ML_HEADS, LANES)],
        out_shape=[jax.ShapeDtypeStruct((ROWS_P, ML_WIDTH), BF),
                   jax.ShapeDtypeStruct((BATCH, ML_HEADS, ML_HEAD_DIM, ML_HEAD_DIM), F32),
                   jax.ShapeDtypeStruct((BATCH, ML_HEADS, ML_HEAD_DIM), F32),
                   jax.ShapeDtypeStruct((BATCH, ML_HEADS, LANES), F32)],
        compiler_params=_cp(("parallel", "arbitrary")),
        name="mlstm_prompt",
    )(proj, proj, proj, proj, gates, bias_row, ng_row)
    return y, cst, nst, mst[:, :, 0]


def _mlstm_sample(proj_bm, gates_bm, bias_row, ng_row, c0, n0, m0, layer):
    L = 128
    body = functools.partial(_mlstm_body, chunk=L, n_valid=DEC_SEQ, rows_in=DEC_SEQ, has_init=True)
    col = lambda off: pl.BlockSpec((None, DEC_SEQ, ML_WIDTH), lambda b, c: (b, 0, off))
    st_in = lambda *tail: pl.BlockSpec((None, None) + tail, lambda b, c: (layer, b) + (0,) * len(tail))
    st = lambda *tail: pl.BlockSpec((None,) + tail, lambda b, c: (b,) + (0,) * len(tail))
    m0b = jnp.broadcast_to(m0[..., None], m0.shape + (LANES,))
    y, cst, nst, mst = pl.pallas_call(
        body,
        grid=(DEC_BATCH, 1),
        in_specs=[col(1), col(2), col(3), col(4),
                  pl.BlockSpec((None, DEC_SEQ, LANES), lambda b, c: (b, 0, 0)),
                  pl.BlockSpec((1, LANES), lambda b, c: (0, 0)),
                  pl.BlockSpec((1, ML_WIDTH), lambda b, c: (0, 0)),
                  st_in(ML_HEADS, ML_HEAD_DIM, ML_HEAD_DIM), st_in(ML_HEADS, ML_HEAD_DIM), st_in(ML_HEADS, LANES)],
        out_specs=[pl.BlockSpec((None, DEC_SEQ, ML_WIDTH), lambda b, c: (b, 0, 0)),
                   st(ML_HEADS, ML_HEAD_DIM, ML_HEAD_DIM), st(ML_HEADS, ML_HEAD_DIM), st(ML_HEADS, LANES)],
        out_shape=[jax.ShapeDtypeStruct((DEC_BATCH, DEC_SEQ, ML_WIDTH), BF),
                   jax.ShapeDtypeStruct((DEC_BATCH, ML_HEADS, ML_HEAD_DIM, ML_HEAD_DIM), F32),
                   jax.ShapeDtypeStruct((DEC_BATCH, ML_HEADS, ML_HEAD_DIM), F32),
                   jax.ShapeDtypeStruct((DEC_BATCH, ML_HEADS, LANES), F32)],
        scratch_shapes=[pltpu.VMEM((L, ML_WIDTH), F32)] * 4 + [pltpu.VMEM((L, LANES), F32)],
        compiler_params=_cp(("arbitrary", "arbitrary")),
        name="mlstm_sample",
    )(proj_bm, proj_bm, proj_bm, proj_bm, gates_bm, bias_row, ng_row, c0, n0, m0b)
    return y, cst, nst, mst[:, :, 0]


def _proj_ln_body(*refs, n_in):
    a_refs = refs[:n_in]
    w_refs = refs[n_in:2 * n_in]
    x_ref, g_ref, b_ref, o_ref = refs[2 * n_in:]
    y = _dot(a_refs[0][...].astype(BF), w_refs[0][...])
    for a_ref, w_ref in zip(a_refs[1:], w_refs[1:]):
        y = y + _dot(a_ref[...].astype(BF), w_ref[...])
    o_ref[...] = _layer_norm(ALPHA * x_ref[...] + y, g_ref[...], b_ref[...])


def _proj_ln(acts, w_b, layer, x, g_row, b_row, tm):
    rows = x.shape[0]
    n_in = len(acts)
    a_specs, w_specs = [], []
    for a in acts:
        kk = a.shape[1]
        slab = len(w_specs)
        a_specs.append(pl.BlockSpec((tm, kk), lambda i: (i, 0)))
        w_specs.append(pl.BlockSpec((None, kk, D_MODEL), lambda i, slab=slab: (layer, slab, 0)))
    row_spec = pl.BlockSpec((1, D_MODEL), lambda i: (0, 0))
    return pl.pallas_call(
        functools.partial(_proj_ln_body, n_in=n_in),
        grid=(rows // tm,),
        in_specs=a_specs + w_specs + [pl.BlockSpec((tm, D_MODEL), lambda i: (i, 0)), row_spec, row_spec],
        out_specs=pl.BlockSpec((tm, D_MODEL), lambda i: (i, 0)),
        out_shape=jax.ShapeDtypeStruct((rows, D_MODEL), F32),
        compiler_params=_cp(("parallel",)),
        name="proj_ln",
    )(*acts, *([w_b] * n_in), x, g_row, b_row)


HALO = BF16_ROWS


def _conv3(up, cw, cb, lo, n):
    r1 = pltpu.roll(up, 1, 0)
    r2 = pltpu.roll(up, 2, 0)
    return cb + cw[2:3, :] * up[lo:lo + n] + cw[1:2, :] * r1[lo:lo + n] + cw[0:1, :] * r2[lo:lo + n]


def _ffn_prompt_body(x_ref, xh_ref, wv_ref, wg_ref, cwv_ref, cwg_ref, cbv_ref, cbg_ref, wd_ref,
                     g_ref, b_ref, o_ref, xb_s, acc_s, *, tm, tiles_per_seq):
    i = pl.program_id(0)
    j = pl.program_id(1)

    @pl.when(j == 0)
    def _():
        first = (i % tiles_per_seq) == 0
        halo = jnp.where(first, 0.0, xh_ref[...])
        xb_s[0:HALO, :] = halo.astype(BF)
        xb_s[HALO:, :] = x_ref[...].astype(BF)
        acc_s[...] = jnp.zeros_like(acc_s)

    xb = xb_s[...]
    hv = _conv3(_dot(xb, wv_ref[...]), cwv_ref[...], cbv_ref[...], HALO, tm)
    hg = _conv3(_dot(xb, wg_ref[...]), cwg_ref[...], cbg_ref[...], HALO, tm)
    act = hg * _sigmoid(hg) * hv
    acc_s[...] += _dot(act.astype(BF), wd_ref[...])

    @pl.when(j == pl.num_programs(1) - 1)
    def _():
        o_ref[...] = _layer_norm(ALPHA * x_ref[...] + acc_s[...], g_ref[...], b_ref[...])


def _ffn_prompt(x, w_up_b, conv_w, conv_b, w_down_b, layer, g_row, b_row):
    tm, tf = 704, 512
    nf = D_FF // tf
    tiles_per_seq = T_PAD // tm
    body = functools.partial(_ffn_prompt_body, tm=tm, tiles_per_seq=tiles_per_seq)
    halo_blocks = tm // HALO
    row_spec = pl.BlockSpec((1, D_MODEL), lambda i, j: (0, 0))
    return pl.pallas_call(
        body,
        grid=(ROWS_P // tm, nf),
        in_specs=[pl.BlockSpec((tm, D_MODEL), lambda i, j: (i, 0)),
                  pl.BlockSpec((HALO, D_MODEL), lambda i, j: (jnp.maximum(i * halo_blocks - 1, 0), 0)),
                  pl.BlockSpec((None, D_MODEL, tf), lambda i, j: (layer, 0, j)),
                  pl.BlockSpec((None, D_MODEL, tf), lambda i, j: (layer, 0, j + nf)),
                  pl.BlockSpec((None, 3, tf), lambda i, j: (layer, 0, j)),
                  pl.BlockSpec((None, 3, tf), lambda i, j: (layer, 0, j + nf)),
                  pl.BlockSpec((None, 1, tf), lambda i, j: (layer, 0, j)),
                  pl.BlockSpec((None, 1, tf), lambda i, j: (layer, 0, j + nf)),
                  pl.BlockSpec((None, tf, D_MODEL), lambda i, j: (layer, j, 0)),
                  row_spec, row_spec],
        out_specs=pl.BlockSpec((tm, D_MODEL), lambda i, j: (i, 0)),
        out_shape=jax.ShapeDtypeStruct((ROWS_P, D_MODEL), F32),
        scratch_shapes=[pltpu.VMEM((tm + HALO, D_MODEL), BF), pltpu.VMEM((tm, D_MODEL), F32)],
        compiler_params=_cp(("parallel", "arbitrary")),
        name="ffn_prompt",
    )(x, x, w_up_b, w_up_b, conv_w, conv_w, conv_b, conv_b, w_down_b, g_row, b_row)


EXTRA = SUBLANES


def _ffn_sample_body(x_ref, p0v_ref, p0g_ref, p1v_ref, p1g_ref, wv_ref, wg_ref, cwv_ref, cwg_ref,
                     cbv_ref, cbg_ref, wd_ref, g_ref, b_ref, o_ref, cs_ref, cp_ref, xb_s, acc_s):
    j = pl.program_id(0)

    @pl.when(j == 0)
    def _():
        xb_s[...] = x_ref[...].astype(BF)
        acc_s[...] = jnp.zeros_like(acc_s)

    xb = xb_s[...]
    n = ROWS_S
    bsz = DEC_BATCH

    def conv(up, p0_ref, p1_ref, cw_ref, cb_ref):
        cw = cw_ref[...]
        ext = jnp.concatenate([p0_ref[...], p1_ref[...], up[0:n]], axis=0)
        return (cb_ref[...] + cw[0:1, :] * ext[0:n] + cw[1:2, :] * ext[bsz:bsz + n]
                + cw[2:3, :] * ext[2 * bsz:2 * bsz + n])

    upv = _dot(xb, wv_ref[...])
    upg = _dot(xb, wg_ref[...])
    hv = conv(upv, p0v_ref, p1v_ref, cwv_ref, cbv_ref)
    hg = conv(upg, p0g_ref, p1g_ref, cwg_ref, cbg_ref)
    act = hg * _sigmoid(hg) * hv
    acc_s[...] += _dot(act.astype(BF), wd_ref[...])
    lo = (DEC_SEQ - 2) * bsz
    cs_ref[0] = upv[lo:lo + bsz]
    cs_ref[1] = upg[lo:lo + bsz]
    cs_ref[2] = upv[lo + bsz:lo + 2 * bsz]
    cs_ref[3] = upg[lo + bsz:lo + 2 * bsz]
    cp_ref[0] = upv[n:n + EXTRA]
    cp_ref[1] = upg[n:n + EXTRA]

    @pl.when(j == pl.num_programs(0) - 1)
    def _():
        o_ref[...] = _layer_norm(ALPHA * x_ref[0:n, :] + acc_s[...], g_ref[...], b_ref[...])


def _ffn_sample(x_ext, prev, w_up_b, conv_w, conv_b, w_down_b, layer, g_row, b_row):
    tf = 512
    nf = D_FF // tf
    rows = ROWS_S + EXTRA
    prev_spec = lambda q: pl.BlockSpec((None, DEC_BATCH, tf), lambda j: (layer, 0, j + q * nf))
    row_spec = pl.BlockSpec((1, D_MODEL), lambda j: (0, 0))
    return pl.pallas_call(
        _ffn_sample_body,
        grid=(nf,),
        in_specs=[pl.BlockSpec((rows, D_MODEL), lambda j: (0, 0)),
                  prev_spec(0), prev_spec(1), prev_spec(2), prev_spec(3),
                  pl.BlockSpec((None, D_MODEL, tf), lambda j: (layer, 0, j)),
                  pl.BlockSpec((None, D_MODEL, tf), lambda j: (layer, 0, j + nf)),
                  pl.BlockSpec((None, 3, tf), lambda j: (layer, 0, j)),
                  pl.BlockSpec((None, 3, tf), lambda j: (layer, 0, j + nf)),
                  pl.BlockSpec((None, 1, tf), lambda j: (layer, 0, j)),
                  pl.BlockSpec((None, 1, tf), lambda j: (layer, 0, j + nf)),
                  pl.BlockSpec((None, tf, D_MODEL), lambda j: (layer, j, 0)),
                  row_spec, row_spec],
        out_specs=[pl.BlockSpec((ROWS_S, D_MODEL), lambda j: (0, 0)),
                   pl.BlockSpec((4, DEC_BATCH, tf), lambda j: (0, 0, j)),
                   pl.BlockSpec((2, EXTRA, tf), lambda j: (0, 0, j))],
        out_shape=[jax.ShapeDtypeStruct((ROWS_S, D_MODEL), F32),
                   jax.ShapeDtypeStruct((4, DEC_BATCH, D_FF), F32),
                   jax.ShapeDtypeStruct((2, EXTRA, D_FF), F32)],
        scratch_shapes=[pltpu.VMEM((rows, D_MODEL), BF), pltpu.VMEM((ROWS_S, D_MODEL), F32)],
        compiler_params=_cp(("arbitrary",)),
        name="ffn_sample",
    )(x_ext, prev, prev, prev, prev, w_up_b, w_up_b, conv_w, conv_w, conv_b, conv_b, w_down_b, g_row, b_row)


def _rms(xf, g):
    return xf * lax.rsqrt(jnp.mean(xf * xf, axis=-1, keepdims=True) + RMS_EPS) * g


def _mla_in_body(x_ref, win_ref, gq_ref, gkv_ref, wuq_ref, cos_ref, sin_ref,
                 ckv_ref, kk_ref, qn_ref, qp_ref):
    p = _dot(x_ref[...].astype(BF), win_ref[...])
    cq = _rms(p[:, 0:Q_LORA], gq_ref[...])
    ckv_ref[...] = _rms(p[:, Q_LORA:Q_LORA + KV_LORA], gkv_ref[...])
    cos = cos_ref[...]
    sin = sin_ref[...]
    o = Q_LORA + KV_LORA
    kk_ref[...] = p[:, o:o + LANES] * cos + p[:, o + LANES:o + 2 * LANES] * sin
    q = _dot(cq.astype(BF), wuq_ref[...])
    n_nope = MLA_HEADS * QK_NOPE
    n_pe = MLA_HEADS * QK_ROPE
    qn_ref[...] = q[:, 0:n_nope].astype(BF)
    reps = n_pe // LANES
    cos_w = jnp.concatenate([cos] * reps, axis=1)
    sin_w = jnp.concatenate([sin] * reps, axis=1)
    qp_ref[...] = (q[:, n_nope:n_nope + n_pe] * cos_w + q[:, n_nope + n_pe:] * sin_w).astype(BF)


def _mla_in(x, win_b, gq_row, gkv_row, wuq_b, cos_t, sin_t, layer, tm):
    rows = x.shape[0]
    n_in = win_b.shape[-1]
    n_q = wuq_b.shape[-1]
    row = lambda w: pl.BlockSpec((tm, w), lambda i: (i, 0))
    return pl.pallas_call(
        _mla_in_body,
        grid=(rows // tm,),
        in_specs=[row(D_MODEL),
                  pl.BlockSpec((None, D_MODEL, n_in), lambda i: (layer, 0, 0)),
                  pl.BlockSpec((1, Q_LORA), lambda i: (0, 0)),
                  pl.BlockSpec((1, KV_LORA), lambda i: (0, 0)),
                  pl.BlockSpec((None, Q_LORA, n_q), lambda i: (layer, 0, 0)),
                  row(LANES), row(LANES)],
        out_specs=[row(KV_LORA), row(LANES), row(MLA_HEADS * QK_NOPE), row(MLA_HEADS * QK_ROPE)],
        out_shape=[jax.ShapeDtypeStruct((rows, KV_LORA), F32),
                   jax.ShapeDtypeStruct((rows, LANES), F32),
                   jax.ShapeDtypeStruct((rows, MLA_HEADS * QK_NOPE), BF),
                   jax.ShapeDtypeStruct((rows, MLA_HEADS * QK_ROPE), BF)],
        compiler_params=_cp(("parallel",)),
        name="mla_in",
    )(x, win_b, gq_row, gkv_row, wuq_b, cos_t, sin_t)


def _fold(x, op):
    out = x[:, 0:LANES]
    for c in range(1, x.shape[1] // LANES):
        out = op(out, x[:, c * LANES:(c + 1) * LANES])
    return out


def _flash_body(qn_ref, qp_ref, kn_ref, kk_ref, v_ref, o_ref, s_s, mx_s, l_s, acc_s, *, tq):
    qi = pl.program_id(2)
    lane = lax.broadcasted_iota(jnp.int32, (1, LANES), 1)
    qp = qp_ref[...].astype(F32)
    qn = qn_ref[...]
    half = LANES // 2
    qs = (jnp.concatenate([qn[:, 0:QK_NOPE], jnp.where(lane < half, qp, 0.0).astype(BF)], axis=1),
          jnp.concatenate([qn[:, QK_NOPE:], jnp.where(lane >= half, qp, 0.0).astype(BF)], axis=1))
    mx_s[...] = jnp.full_like(mx_s, NEG)
    l_s[...] = jnp.zeros_like(l_s)
    acc_s[...] = jnp.zeros_like(acc_s)
    reps = tq // LANES

    def logits(kc, masked):
        off = pl.multiple_of(kc * tq, tq)
        kn = kn_ref[pl.ds(off, tq), :]
        kk = kk_ref[pl.ds(off, tq), :].astype(BF)
        for hh in range(2):
            kh = jnp.concatenate([kn[:, hh * QK_NOPE:(hh + 1) * QK_NOPE], kk], axis=1)
            s = _dot_nt(qs[hh], kh) * (MLA_SCALE * LOG2E)
            if masked:
                ti = lax.broadcasted_iota(jnp.int32, (tq, tq), 0)
                si = lax.broadcasted_iota(jnp.int32, (tq, tq), 1)
                s = jnp.where(si <= ti, s, NEG)
            s_s[hh, kc] = s
            mx_s[hh] = jnp.maximum(mx_s[hh], _fold(s, jnp.maximum))

    def pairs(n, fn):
        def two(p, carry):
            fn(2 * p)
            fn(2 * p + 1)
            return carry

        lax.fori_loop(0, n // 2, two, 0)

        @pl.when(n % 2 == 1)
        def _():
            fn(n - 1)

    pairs(qi, lambda kc: logits(kc, False))
    logits(qi, True)

    m_rows = [jnp.concatenate([jnp.broadcast_to(jnp.max(mx_s[hh], axis=1, keepdims=True), (tq, LANES))] * reps,
                              axis=1) for hh in range(2)]

    def weighted(kc):
        off = pl.multiple_of(kc * tq, tq)
        vv = v_ref[pl.ds(off, tq), :]
        for hh in range(2):
            p = jnp.exp2(s_s[hh, kc] - m_rows[hh])
            l_s[hh] += _fold(p, jnp.add)
            acc_s[hh] += _dot(p.astype(BF), vv[:, hh * V_DIM:(hh + 1) * V_DIM])

    pairs(qi + 1, weighted)
    outs = [acc_s[hh] / jnp.sum(l_s[hh], axis=1, keepdims=True) for hh in range(2)]
    o_ref[...] = jnp.concatenate(outs, axis=1).astype(o_ref.dtype)


def _flash(qn, qp, kn, kk, v):
    tq = 384
    nq = T_PAD // tq
    body = functools.partial(_flash_body, tq=tq)
    two = 2 * QK_NOPE
    return pl.pallas_call(
        body,
        grid=(BATCH, MLA_HEADS // 2, nq),
        in_specs=[pl.BlockSpec((tq, two), lambda b, h, q: (b * nq + q, h)),
                  pl.BlockSpec((tq, LANES), lambda b, h, q: (b * nq + q, h)),
                  pl.BlockSpec((T_PAD, two), lambda b, h, q: (b, h)),
                  pl.BlockSpec((T_PAD, LANES), lambda b, h, q: (b, 0)),
                  pl.BlockSpec((T_PAD, two), lambda b, h, q: (b, h))],
        out_specs=pl.BlockSpec((tq, two), lambda b, h, q: (b * nq + q, h)),
        out_shape=jax.ShapeDtypeStruct((ROWS_P, MLA_HEADS * V_DIM), BF),
        scratch_shapes=[pltpu.VMEM((2, nq, tq, tq), F32), pltpu.VMEM((2, tq, LANES), F32),
                        pltpu.VMEM((2, tq, LANES), F32), pltpu.VMEM((2, tq, V_DIM), F32)],
        compiler_params=_cp(("parallel", "parallel", "arbitrary")),
        name="mla_flash",
    )(qn, qp, kn, kk, v)


def _absorb_body(qn_ref, wuk_ref, o_ref):
    o_ref[...] = _dot_nt(qn_ref[...], wuk_ref[...]).astype(o_ref.dtype)


def _absorb(qn, wuk_b, layer):
    return pl.pallas_call(
        _absorb_body,
        grid=(MLA_HEADS,),
        in_specs=[pl.BlockSpec((ROWS_S, QK_NOPE), lambda h: (0, h)),
                  pl.BlockSpec((None, KV_LORA, QK_NOPE), lambda h: (layer, 0, h))],
        out_specs=pl.BlockSpec((ROWS_S, KV_LORA), lambda h: (0, h)),
        out_shape=jax.ShapeDtypeStruct((ROWS_S, MLA_HEADS * KV_LORA), BF),
        compiler_params=_cp(("parallel",)),
        name="mla_absorb",
    )(qn, wuk_b)


def _unabsorb_body(ol_ref, wuv_ref, o_ref):
    o_ref[...] = _dot(ol_ref[...], wuv_ref[...]).astype(o_ref.dtype)


def _unabsorb(ol, wuv_b, layer):
    return pl.pallas_call(
        _unabsorb_body,
        grid=(MLA_HEADS,),
        in_specs=[pl.BlockSpec((ROWS_S, KV_LORA), lambda h: (0, h)),
                  pl.BlockSpec((None, KV_LORA, V_DIM), lambda h: (layer, 0, h))],
        out_specs=pl.BlockSpec((ROWS_S, V_DIM), lambda h: (0, h)),
        out_shape=jax.ShapeDtypeStruct((ROWS_S, MLA_HEADS * V_DIM), BF),
        compiler_params=_cp(("parallel",)),
        name="mla_unabsorb",
    )(ol, wuv_b)


QROWS = DEC_SEQ * MLA_HEADS
AHEAD = 2


def _paged_body(pt_ref, ql_ref, qp_ref, cn_ref, kn_ref, ckv_hbm, kpe_hbm, o_ref,
                cbuf, kbuf, sem, m_s, l_s, acc_s, newc_s, newk_s, *, layer, n_chunks, n_pages):
    b = pl.program_id(0)
    page = LANES

    def chunk_copies(seq, c, slot, known_pages):
        out = []
        for i in range(n_pages):
            pid = pt_ref[seq, c * n_pages + i] if known_pages else 0
            out.append(pltpu.make_async_copy(ckv_hbm.at[layer, pid],
                                             cbuf.at[slot, pl.ds(i * page, page), :], sem.at[slot]))
            out.append(pltpu.make_async_copy(kpe_hbm.at[layer, pid],
                                             kbuf.at[slot, :, pl.ds(i * page, page)], sem.at[slot]))
        return out

    @pl.when(b == 0)
    def _():
        for c in range(AHEAD):
            for cp in chunk_copies(0, c, c, True):
                cp.start()

    m_s[...] = jnp.full_like(m_s, NEG)
    l_s[...] = jnp.zeros_like(l_s)
    acc_s[...] = jnp.zeros_like(acc_s)

    ql = ql_ref[...].reshape(QROWS, KV_LORA)
    qp = qp_ref[...].reshape(QROWS, QK_ROPE)

    def local_softmax(s, vb):
        m = jnp.max(s, axis=1, keepdims=True)
        p = jnp.exp(s - m)
        return m, jnp.sum(p, axis=1, keepdims=True), _dot(p.astype(BF), vb)

    def merge(parts):
        m_prev = m_s[...]
        m_new = m_prev
        for m, _, _ in parts:
            m_new = jnp.maximum(m_new, m)
        alpha = jnp.exp(m_prev - m_new)
        l = alpha * l_s[...]
        acc = alpha * acc_s[...]
        for m, lg, ag in parts:
            w = jnp.exp(m - m_new)
            l = l + w * lg
            acc = acc + w * ag
        m_s[...] = m_new
        l_s[...] = l
        acc_s[...] = acc

    for c in range(n_chunks):
        for cp in chunk_copies(b, c, c, False):
            cp.wait()
        nxt = c + AHEAD
        if nxt < n_chunks:
            for cp in chunk_copies(b, nxt, nxt, True):
                cp.start()
        else:
            @pl.when(b + 1 < pl.num_programs(0))
            def _():
                for cp in chunk_copies(b + 1, nxt - n_chunks, nxt - n_chunks, True):
                    cp.start()
        cb = cbuf[c].astype(BF)
        kt = kbuf[c].astype(BF)
        merge([local_softmax((_dot_nt(ql, cb) + _dot(qp, kt)) * MLA_SCALE, cb)])

    newc_s[...] = jnp.zeros_like(newc_s)
    newk_s[...] = jnp.zeros_like(newk_s)
    newc_s[0:DEC_SEQ, :] = cn_ref[...]
    newk_s[0:DEC_SEQ, :] = kn_ref[...]
    cn = newc_s[...].astype(BF)
    s = (_dot_nt(ql, cn) + _dot_nt(qp, newk_s[...].astype(BF))) * MLA_SCALE
    qt = lax.broadcasted_iota(jnp.int32, (QROWS, LANES), 0) // MLA_HEADS
    kt = lax.broadcasted_iota(jnp.int32, (QROWS, LANES), 1)
    merge([local_softmax(jnp.where(kt <= qt, s, NEG), cn)])
    o = acc_s[...] / l_s[...]
    o_ref[...] = o.reshape(DEC_SEQ, MLA_HEADS, KV_LORA).astype(o_ref.dtype)


def _paged_attention(page_table, q_lat, q_pe, ckv_new, kpe_new, cache_ckv, cache_kpe_t, layer):
    n_pages = PAGES_PER_STEP
    n_chunks = page_table.shape[1] // n_pages
    assert n_chunks > AHEAD and cache_ckv.shape[2] == LANES
    keys = n_pages * LANES
    body = functools.partial(_paged_body, layer=layer, n_chunks=n_chunks, n_pages=n_pages)
    qspec = lambda w: pl.BlockSpec((DEC_SEQ, None, MLA_HEADS, w), lambda b, pt: (0, b, 0, 0))
    nspec = lambda w: pl.BlockSpec((None, DEC_SEQ, w), lambda b, pt: (b, 0, 0))
    grid_spec = pltpu.PrefetchScalarGridSpec(
        num_scalar_prefetch=1,
        grid=(DEC_BATCH,),
        in_specs=[qspec(KV_LORA), qspec(QK_ROPE), nspec(KV_LORA), nspec(QK_ROPE),
                  pl.BlockSpec(memory_space=pl.ANY), pl.BlockSpec(memory_space=pl.ANY)],
        out_specs=pl.BlockSpec((DEC_SEQ, None, MLA_HEADS, KV_LORA), lambda b, pt: (0, b, 0, 0)),
        scratch_shapes=[pltpu.VMEM((n_chunks, keys, KV_LORA), F32), pltpu.VMEM((n_chunks, QK_ROPE, keys), F32),
                        pltpu.SemaphoreType.DMA((n_chunks,)),
                        pltpu.VMEM((QROWS, 1), F32), pltpu.VMEM((QROWS, 1), F32),
                        pltpu.VMEM((QROWS, KV_LORA), F32),
                        pltpu.VMEM((LANES, KV_LORA), F32), pltpu.VMEM((LANES, QK_ROPE), F32)],
    )
    return pl.pallas_call(
        body,
        grid_spec=grid_spec,
        out_shape=jax.ShapeDtypeStruct((DEC_SEQ, DEC_BATCH, MLA_HEADS, KV_LORA), BF),
        compiler_params=_cp(("arbitrary",)),
        name="mla_paged",
    )(page_table, q_lat, q_pe, ckv_new, kpe_new, cache_ckv, cache_kpe_t)


def _rope_tables(pos):
    half = QK_ROPE // 2
    inv = ROPE_THETA ** (-jnp.arange(half, dtype=F32) / half)
    ang = pos.astype(F32)[:, None] * inv[None, :]
    cos, sin = jnp.cos(ang), jnp.sin(ang)
    return (jnp.concatenate([cos, cos, cos, cos], axis=1),
            jnp.concatenate([-sin, sin, -sin, sin], axis=1))


def _swap_halves(w):
    half = w.shape[-1] // 2
    return jnp.concatenate([w[..., half:], w[..., :half]], axis=-1)


def kernel(x_prompt, x_sample, cache_mla_ckv, cache_mla_kpe, page_table, state_s5_re, state_s5_im, state_mlstm_c, state_mlstm_n, state_mlstm_m, state_ffn_conv, meta_tokens, ln1_g, ln1_b, ln2_g, ln2_b, mix_w_in, mix_b_gates, s5_a_re, s5_a_im, s5_log_dt, s5_b_re, s5_b_im, s5_c_re, s5_c_im, s5_d, s5_w_glu, s5_b_glu, ml_norm_g, mix_w_out, mla_w_in, mla_q_norm_g, mla_kv_norm_g, mla_w_uq, mla_w_uk, mla_w_uv, mla_w_out, ffn_w_up, ffn_conv_w, ffn_conv_b, ffn_w_down):
    n_mix = mix_w_in.shape[0]
    n_mla = mla_w_in.shape[0]
    past = page_table.shape[1] * cache_mla_ckv.shape[2]

    n_main = S5_WIDTH + 4 * ML_WIDTH
    mix_w_in_b = mix_w_in.astype(BF)
    mix_w_g_b = jnp.pad(mix_w_in[:, :, n_main:], ((0, 0), (0, 0), (0, LANES - 2 * ML_HEADS))).astype(BF)
    gate_bias = jnp.pad(mix_b_gates, ((0, 0), (0, LANES - 2 * ML_HEADS)))
    w_glu_b = s5_w_glu.astype(BF)
    mix_w_out_b = mix_w_out.astype(BF)
    o = Q_LORA + KV_LORA
    w_kpe = mla_w_in[:, :, o:]
    w_kpe_sw = _swap_halves(w_kpe)
    mla_w_in_b = jnp.concatenate([mla_w_in[:, :, :o], w_kpe, w_kpe, w_kpe_sw, w_kpe_sw], axis=-1).astype(BF)
    wq4 = mla_w_uq.reshape(n_mla, Q_LORA, MLA_HEADS, QK_NOPE + QK_ROPE)
    wq_pe = wq4[..., QK_NOPE:]
    mla_w_uq_b = jnp.concatenate(
        [wq4[..., :QK_NOPE].reshape(n_mla, Q_LORA, -1), wq_pe.reshape(n_mla, Q_LORA, -1),
         _swap_halves(wq_pe).reshape(n_mla, Q_LORA, -1)], axis=-1).astype(BF)
    mla_w_uk_b = mla_w_uk.reshape(n_mla, KV_LORA, MLA_HEADS * QK_NOPE).astype(BF)
    mla_w_uv_b = mla_w_uv.reshape(n_mla, KV_LORA, MLA_HEADS * V_DIM).astype(BF)
    mla_w_out_b = mla_w_out.astype(BF)
    ffn_w_up_b = ffn_w_up.astype(BF)
    ffn_w_down_b = ffn_w_down.astype(BF)
    conv_b3 = ffn_conv_b[:, None, :]
    prev_conv = state_ffn_conv.reshape(DEPTH, DEC_BATCH, 4 * D_FF)
    cache_kpe_t = jnp.swapaxes(cache_mla_kpe, 2, 3)
    cos_p, sin_p = _rope_tables(jnp.arange(T_PAD))
    cos_p = jnp.concatenate([cos_p] * BATCH, axis=0)
    sin_p = jnp.concatenate([sin_p] * BATCH, axis=0)
    cos_s, sin_s = _rope_tables(past + jnp.arange(DEC_SEQ))
    cos_s = jnp.repeat(cos_s, DEC_BATCH, axis=0)
    sin_s = jnp.repeat(sin_s, DEC_BATCH, axis=0)

    xp = jnp.concatenate([jnp.broadcast_to(meta_tokens[None], (BATCH, N_META, D_MODEL)), x_prompt], axis=1)
    xp = jnp.pad(xp, ((0, 0), (0, T_PAD - T_REAL), (0, 0))).reshape(ROWS_P, D_MODEL)
    xs = x_sample.transpose(1, 0, 2).reshape(ROWS_S, D_MODEL)
    tm_p, tm_s = 704, ROWS_S

    outs = {k: [] for k in ("ckv_p", "kpe_p", "ckv_s", "kpe_s", "s5r_p", "s5i_p", "s5r_s", "s5i_s",
                            "mc_p", "mn_p", "mm_p", "mc_s", "mn_s", "mm_s", "conv_p", "conv_s")}

    for l in range(DEPTH):
        j = l // 2
        g1, b1 = ln1_g[l][None, :], ln1_b[l][None, :]
        g2, b2 = ln2_g[l][None, :], ln2_b[l][None, :]
        if l % 2 == 0:
            ab_re, ab_im, bb_re, bb_im = _s5_params(s5_a_re[j], s5_a_im[j], s5_log_dt[j], s5_b_re[j], s5_b_im[j])
            bre, bim = _blockdiag_in(bb_re), _blockdiag_in(bb_im)
            cre, cim = _blockdiag_out(s5_c_re[j]), _blockdiag_out(s5_c_im[j])
            d_row = s5_d[j].reshape(1, S5_WIDTH)
            bias_row = gate_bias[j][None, :]
            ng_row = ml_norm_g[j][None, :]
            bglu_row = s5_b_glu[j][None, :]

            proj, gates = _mix_proj(xp, mix_w_in_b, mix_w_g_b, j, tm_p)
            y5, hr, hi = _s5_prompt(proj, bre, bim, cre, cim, d_row, ab_re, ab_im)
            y5 = _glu(y5, w_glu_b, bglu_row, j, tm_p)
            yml, c1, n1, m1 = _mlstm_prompt(proj, gates, bias_row, ng_row)
            xp = _proj_ln([y5, yml], mix_w_out_b, j, xp, g1, b1, 352)
            outs["s5r_p"].append(hr); outs["s5i_p"].append(hi)
            outs["mc_p"].append(c1); outs["mn_p"].append(n1); outs["mm_p"].append(m1)

            proj, gates = _mix_proj(xs, mix_w_in_b, mix_w_g_b, j, tm_s)
            y5, hr, hi = _s5_sample(proj, state_s5_re[j], state_s5_im[j], bre, bim, cre, cim, d_row, ab_re, ab_im)
            y5 = _glu(y5, w_glu_b, bglu_row, j, tm_s)
            proj_bm = proj.reshape(DEC_SEQ, DEC_BATCH, n_main).transpose(1, 0, 2)
            gates_bm = gates.reshape(DEC_SEQ, DEC_BATCH, LANES).transpose(1, 0, 2)
            yml, c1, n1, m1 = _mlstm_sample(proj_bm, gates_bm, bias_row, ng_row,
                                            state_mlstm_c, state_mlstm_n, state_mlstm_m, j)
            yml = yml.transpose(1, 0, 2).reshape(ROWS_S, ML_WIDTH)
            xs = _proj_ln([y5, yml], mix_w_out_b, j, xs, g1, b1, 256)
            outs["s5r_s"].append(hr); outs["s5i_s"].append(hi)
            outs["mc_s"].append(c1); outs["mn_s"].append(n1); outs["mm_s"].append(m1)
        else:
            gq_row = mla_q_norm_g[j][None, :]
            gkv_row = mla_kv_norm_g[j][None, :]

            ckv, kk, qn, qp = _mla_in(xp, mla_w_in_b, gq_row, gkv_row, mla_w_uq_b, cos_p, sin_p, j, 352)
            kn = _mm(ckv, mla_w_uk_b, j, BF, 384, MLA_HEADS * QK_NOPE)
            vv = _mm(ckv, mla_w_uv_b, j, BF, 384, MLA_HEADS * V_DIM)
            att = _flash(qn, qp, kn, kk, vv)
            xp = _proj_ln([att], mla_w_out_b, j, xp, g1, b1, 352)
            outs["ckv_p"].append(ckv.reshape(BATCH, T_PAD, KV_LORA)[:, :T_REAL])
            outs["kpe_p"].append(kk.reshape(BATCH, T_PAD, LANES)[:, :T_REAL, :QK_ROPE])

            ckv, kk, qn, qp = _mla_in(xs, mla_w_in_b, gq_row, gkv_row, mla_w_uq_b, cos_s, sin_s, j, 256)
            q_lat = _absorb(qn, mla_w_uk_b, j)
            ckv_bm = ckv.reshape(DEC_SEQ, DEC_BATCH, KV_LORA).transpose(1, 0, 2)
            kpe_bm = kk[:, :QK_ROPE].reshape(DEC_SEQ, DEC_BATCH, QK_ROPE).transpose(1, 0, 2)
            o_lat = _paged_attention(
                page_table,
                q_lat.reshape(DEC_SEQ, DEC_BATCH, MLA_HEADS, KV_LORA),
                qp.reshape(DEC_SEQ, DEC_BATCH, MLA_HEADS, QK_ROPE),
                ckv_bm, kpe_bm, cache_mla_ckv, cache_kpe_t, j)
            att = _unabsorb(o_lat.reshape(ROWS_S, MLA_HEADS * KV_LORA), mla_w_uv_b, j)
            xs = _proj_ln([att], mla_w_out_b, j, xs, g1, b1, 256)
            outs["ckv_s"].append(ckv_bm)
            outs["kpe_s"].append(kpe_bm)

        tail = jnp.concatenate([xp[b * T_PAD + T_REAL - 2:b * T_PAD + T_REAL] for b in range(BATCH)]
                               + [jnp.zeros((EXTRA - 2 * BATCH, D_MODEL), F32)], axis=0)
        xs_ext = jnp.concatenate([xs, tail], axis=0)
        xs, cs, cpv = _ffn_sample(xs_ext, prev_conv, ffn_w_up_b, ffn_conv_w, conv_b3, ffn_w_down_b, l, g2, b2)
        xp = _ffn_prompt(xp, ffn_w_up_b, ffn_conv_w, conv_b3, ffn_w_down_b, l, g2, b2)
        outs["conv_s"].append(cs.transpose(1, 0, 2).reshape(DEC_BATCH, 2, 2 * D_FF))
        cp2 = jnp.concatenate([cpv[0, :2 * BATCH], cpv[1, :2 * BATCH]], axis=-1)
        outs["conv_p"].append(cp2.reshape(BATCH, 2, 2 * D_FF))

    st = lambda k: jnp.stack(outs[k])
    y_prompt = xp.reshape(BATCH, T_PAD, D_MODEL)[:, N_META:T_REAL]
    y_sample = xs.reshape(DEC_SEQ, DEC_BATCH, D_MODEL).transpose(1, 0, 2)
    return (y_prompt, y_sample,
            st("ckv_p"), st("kpe_p"), st("ckv_s"), st("kpe_s"),
            st("s5r_p"), st("s5i_p"), st("s5r_s"), st("s5i_s"),
            st("mc_p"), st("mn_p"), st("mm_p"), st("mc_s"), st("mn_s"), st("mm_s"),
            st("conv_p"), st("conv_s"))
```

```python
import functools
import math

import jax
import jax.numpy as jnp
from jax import lax
from jax.experimental import pallas as pl
from jax.experimental.pallas import tpu as pltpu

BF = jnp.bfloat16
F32 = jnp.float32

D_MODEL = 2048
BATCH = 2
SEQ = 4096
DEPTH = 4
DEC_BATCH = 128
DEC_SEQ = 4
N_META = 16
S5_WIDTH = 1024
S5_GROUP = 16
S5_GROUPS = 64
S5_STATE = 64
ML_WIDTH = 1024
ML_HEADS = 8
ML_HEAD_DIM = 128
MLA_HEADS = 16
Q_LORA = 512
KV_LORA = 512
QK_NOPE = 128
QK_ROPE = 64
V_DIM = 128
ROPE_THETA = 10000.0
MLA_SCALE = (QK_NOPE + QK_ROPE) ** -0.5
D_FF = 5632
ALPHA = (2 * DEPTH) ** 0.25
LN_EPS = 1e-5
RMS_EPS = 1e-6
NEG = -1e30
LOG2E = 1.4426950408889634

LANES = 128
SUBLANES = 8
BF16_ROWS = 16
VMEM_LIMIT = 56 * 1024 * 1024

T_REAL = N_META + SEQ
T_PAD = 4224
ROWS_P = BATCH * T_PAD
ROWS_S = DEC_BATCH * DEC_SEQ
S5_BLK = 512
N_S5_BLK = S5_GROUPS * S5_STATE // S5_BLK
PAGES_PER_STEP = 16


def _cp(sem, vmem=VMEM_LIMIT):
    return pltpu.CompilerParams(dimension_semantics=sem, vmem_limit_bytes=vmem)


def _dot(a, b):
    return jnp.dot(a, b, preferred_element_type=F32)


def _dot_nt(a, b):
    return lax.dot_general(a, b, (((1,), (1,)), ((), ())), preferred_element_type=F32)


def _sigmoid(x):
    return 1.0 / (1.0 + jnp.exp(-x))


def _layer_norm(xf, g, b):
    mu = jnp.mean(xf, axis=-1, keepdims=True)
    xc = xf - mu
    var = jnp.mean(xc * xc, axis=-1, keepdims=True)
    return xc * lax.rsqrt(var + LN_EPS) * g + b


def _mm_body(x_ref, w_ref, o_ref):
    o_ref[...] = _dot(x_ref[...].astype(BF), w_ref[...]).astype(o_ref.dtype)


def _mm(x, w, layer, out_dtype, tm, tn):
    rows, k = x.shape
    n = w.shape[-1]
    return pl.pallas_call(
        _mm_body,
        grid=(rows // tm, n // tn),
        in_specs=[pl.BlockSpec((tm, k), lambda i, j: (i, 0)),
                  pl.BlockSpec((None, k, tn), lambda i, j: (layer, 0, j))],
        out_specs=pl.BlockSpec((tm, tn), lambda i, j: (i, j)),
        out_shape=jax.ShapeDtypeStruct((rows, n), out_dtype),
        compiler_params=_cp(("parallel", "arbitrary")),
        name="mm",
    )(x, w)


def _mix_proj_body(x_ref, w_ref, wg_ref, p_ref, g_ref, xb_s):
    @pl.when(pl.program_id(1) == 0)
    def _():
        xb = x_ref[...].astype(BF)
        xb_s[...] = xb
        g_ref[...] = _dot(xb, wg_ref[...])

    p_ref[...] = _dot(xb_s[...], w_ref[...])


def _mix_proj(x, w_in_b, w_g_b, layer, tm):
    rows = x.shape[0]
    tn = 1024
    n_main = S5_WIDTH + 4 * ML_WIDTH
    return pl.pallas_call(
        _mix_proj_body,
        grid=(rows // tm, n_main // tn),
        in_specs=[pl.BlockSpec((tm, D_MODEL), lambda i, j: (i, 0)),
                  pl.BlockSpec((None, D_MODEL, tn), lambda i, j: (layer, 0, j)),
                  pl.BlockSpec((None, D_MODEL, LANES), lambda i, j: (layer, 0, 0))],
        out_specs=[pl.BlockSpec((tm, tn), lambda i, j: (i, j)),
                   pl.BlockSpec((tm, LANES), lambda i, j: (i, 0))],
        out_shape=[jax.ShapeDtypeStruct((rows, n_main), F32),
                   jax.ShapeDtypeStruct((rows, LANES), F32)],
        scratch_shapes=[pltpu.VMEM((tm, D_MODEL), BF)],
        compiler_params=_cp(("parallel", "arbitrary")),
        name="mix_proj",
    )(x, w_in_b, w_g_b)


def _s5_params_body(are_ref, aim_ref, ldt_ref, bre_ref, bim_ref, abr_ref, abi_ref, bbr_ref, bbi_ref):
    lam_re = are_ref[...]
    lam_im = aim_ref[...]
    dt = jnp.exp(ldt_ref[...])
    mag = jnp.exp(lam_re * dt)
    ab_re = mag * jnp.cos(lam_im * dt)
    ab_im = mag * jnp.sin(lam_im * dt)
    den = lam_re * lam_re + lam_im * lam_im
    z_re = ((ab_re - 1.0) * lam_re + ab_im * lam_im) / den
    z_im = (ab_im * lam_re - (ab_re - 1.0) * lam_im) / den
    br = bre_ref[...]
    bi = bim_ref[...]
    abr_ref[...] = ab_re
    abi_ref[...] = ab_im
    bbr_ref[...] = z_re * br - z_im * bi
    bbi_ref[...] = z_re * bi + z_im * br


def _s5_params(a_re, a_im, log_dt, b_re, b_im):
    n = S5_GROUPS * S5_STATE
    col = lambda a: a.reshape(n, 1)
    ldt = jnp.broadcast_to(log_dt[:, None], (S5_GROUPS, S5_STATE)).reshape(n, 1)
    out = pl.pallas_call(
        _s5_params_body,
        out_shape=[jax.ShapeDtypeStruct((n, 1), F32), jax.ShapeDtypeStruct((n, 1), F32),
                   jax.ShapeDtypeStruct((n, S5_GROUP), F32), jax.ShapeDtypeStruct((n, S5_GROUP), F32)],
        name="s5_params",
    )(col(a_re), col(a_im), ldt, b_re.reshape(n, S5_GROUP), b_im.reshape(n, S5_GROUP))
    ab_re, ab_im, bb_re, bb_im = out
    return ab_re.reshape(1, n), ab_im.reshape(1, n), bb_re, bb_im


def _blockdiag_in(bb):
    nb = N_S5_BLK
    g = S5_GROUPS // nb
    b4 = bb.reshape(nb, g, S5_STATE, S5_GROUP).transpose(0, 1, 3, 2)
    eye = jnp.eye(g, dtype=bool)[None, :, None, :, None]
    out = jnp.where(eye, b4[:, :, :, None, :], 0.0)
    return out.reshape(nb, g * S5_GROUP, g * S5_STATE).astype(BF)


def _blockdiag_out(c):
    nb = N_S5_BLK
    g = S5_GROUPS // nb
    c4 = c.reshape(nb, g, S5_GROUP, S5_STATE).transpose(0, 1, 3, 2)
    eye = jnp.eye(g, dtype=bool)[None, :, None, :, None]
    out = jnp.where(eye, c4[:, :, :, None, :], 0.0)
    return out.reshape(nb, g * S5_STATE, g * S5_GROUP).astype(BF)


def _cmul(ar, ai, br, bi):
    return ar * br - ai * bi, ar * bi + ai * br


def _s5_prompt_body(u_ref, bre_ref, bim_ref, cre_ref, cim_ref, d_ref, ar_ref, ai_ref,
                    y_ref, hre_ref, him_ref, xr_s, xi_s, car_s, cai_s, *, tc, t_final):
    c = pl.program_id(2)

    @pl.when(c == 0)
    def _():
        car_s[...] = jnp.zeros_like(car_s)
        cai_s[...] = jnp.zeros_like(cai_s)

    u = u_ref[...]
    ub = u.astype(BF)
    xr_s[...] = _dot(ub, bre_ref[...])
    xi_s[...] = _dot(ub, bim_ref[...])

    a1 = (ar_ref[...], ai_ref[...])
    a2 = _cmul(*a1, *a1)
    a3 = _cmul(*a2, *a1)
    a4 = _cmul(*a2, *a2)
    a5 = _cmul(*a4, *a1)
    a6 = _cmul(*a4, *a2)
    a7 = _cmul(*a4, *a3)
    a8 = _cmul(*a4, *a4)
    pows = (a1, a2, a3, a4, a5, a6, a7, a8)
    pw_r = jnp.concatenate([p[0] for p in pows], axis=0)
    pw_i = jnp.concatenate([p[1] for p in pows], axis=0)
    sub = lax.broadcasted_iota(jnp.int32, (SUBLANES, S5_BLK), 0)
    steps = [(d, jnp.where(sub >= d, pr, 0.0), jnp.where(sub >= d, pi, 0.0))
             for d, (pr, pi) in ((1, a1), (2, a2), (4, a4))]

    def tile(n, carry):
        cr, ci = carry
        off = pl.multiple_of(n * SUBLANES, SUBLANES)
        xr = xr_s[pl.ds(off, SUBLANES), :]
        xi = xi_s[pl.ds(off, SUBLANES), :]
        for d, pr, pi in steps:
            sr = pltpu.roll(xr, d, 0)
            si = pltpu.roll(xi, d, 0)
            xr, xi = xr + pr * sr - pi * si, xi + pr * si + pi * sr
        hr = xr + pw_r * cr - pw_i * ci
        hi = xi + pw_r * ci + pw_i * cr
        xr_s[pl.ds(off, SUBLANES), :] = hr
        xi_s[pl.ds(off, SUBLANES), :] = hi
        return hr[SUBLANES - 1:SUBLANES, :], hi[SUBLANES - 1:SUBLANES, :]

    cr, ci = lax.fori_loop(0, tc // SUBLANES, tile, (car_s[...], cai_s[...]), unroll=4)
    car_s[...] = cr
    cai_s[...] = ci

    hr = xr_s[...]
    hi = xi_s[...]
    y_ref[...] = _dot(hr.astype(BF), cre_ref[...]) - _dot(hi.astype(BF), cim_ref[...]) + d_ref[...] * u

    @pl.when(c == t_final // tc)
    def _():
        r = t_final % tc
        hre_ref[...] = jnp.broadcast_to(xr_s[r:r + 1, :], hre_ref.shape)
        him_ref[...] = jnp.broadcast_to(xi_s[r:r + 1, :], him_ref.shape)


def _s5_prompt(proj, bre, bim, cre, cim, d_row, ab_re, ab_im):
    tc = 384
    nt = T_PAD // tc
    body = functools.partial(_s5_prompt_body, tc=tc, t_final=T_REAL - 1)
    wspec_in = pl.BlockSpec((None, LANES, S5_BLK), lambda b, s, c: (s, 0, 0))
    wspec_out = pl.BlockSpec((None, S5_BLK, LANES), lambda b, s, c: (s, 0, 0))
    aspec = pl.BlockSpec((1, S5_BLK), lambda b, s, c: (0, s))
    st_spec = pl.BlockSpec((None, None, SUBLANES, S5_BLK), lambda b, s, c: (b, s, 0, 0))
    st_shape = jax.ShapeDtypeStruct((BATCH, N_S5_BLK, SUBLANES, S5_BLK), F32)
    y, hre, him = pl.pallas_call(
        body,
        grid=(BATCH, N_S5_BLK, nt),
        in_specs=[pl.BlockSpec((tc, LANES), lambda b, s, c: (b * nt + c, s)),
                  wspec_in, wspec_in, wspec_out, wspec_out,
                  pl.BlockSpec((1, LANES), lambda b, s, c: (0, s)),
                  aspec, aspec],
        out_specs=[pl.BlockSpec((tc, LANES), lambda b, s, c: (b * nt + c, s)), st_spec, st_spec],
        out_shape=[jax.ShapeDtypeStruct((ROWS_P, S5_WIDTH), F32), st_shape, st_shape],
        scratch_shapes=[pltpu.VMEM((tc, S5_BLK), F32), pltpu.VMEM((tc, S5_BLK), F32),
                        pltpu.VMEM((1, S5_BLK), F32), pltpu.VMEM((1, S5_BLK), F32)],
        compiler_params=_cp(("parallel", "parallel", "arbitrary")),
        name="s5_prompt",
    )(proj, bre, bim, cre, cim, d_row, ab_re, ab_im)
    fin = lambda h: h[:, :, 0, :].reshape(BATCH, S5_GROUPS, S5_STATE)
    return y, fin(hre), fin(him)


def _s5_sample_body(u_ref, h0r_ref, h0i_ref, bre_ref, bim_ref, cre_ref, cim_ref, d_ref, ar_ref, ai_ref,
                    y_ref, hre_ref, him_ref):
    u = u_ref[...]
    ub = u.astype(BF)
    bur = _dot(ub, bre_ref[...])
    bui = _dot(ub, bim_ref[...])
    ar = ar_ref[...]
    ai = ai_ref[...]
    hr = h0r_ref[...]
    hi = h0i_ref[...]
    hrs, his = [], []
    for t in range(DEC_SEQ):
        lo = t * DEC_BATCH
        hr, hi = (ar * hr - ai * hi + bur[lo:lo + DEC_BATCH], ar * hi + ai * hr + bui[lo:lo + DEC_BATCH])
        hrs.append(hr)
        his.append(hi)
    hra = jnp.concatenate(hrs, axis=0)
    hia = jnp.concatenate(his, axis=0)
    y_ref[...] = _dot(hra.astype(BF), cre_ref[...]) - _dot(hia.astype(BF), cim_ref[...]) + d_ref[...] * u
    hre_ref[...] = hr
    him_ref[...] = hi


def _s5_sample(proj, h0_re, h0_im, bre, bim, cre, cim, d_row, ab_re, ab_im):
    n = S5_GROUPS * S5_STATE
    wspec_in = pl.BlockSpec((None, LANES, S5_BLK), lambda s: (s, 0, 0))
    wspec_out = pl.BlockSpec((None, S5_BLK, LANES), lambda s: (s, 0, 0))
    aspec = pl.BlockSpec((1, S5_BLK), lambda s: (0, s))
    hspec = pl.BlockSpec((DEC_BATCH, S5_BLK), lambda s: (0, s))
    y, hre, him = pl.pallas_call(
        _s5_sample_body,
        grid=(N_S5_BLK,),
        in_specs=[pl.BlockSpec((ROWS_S, LANES), lambda s: (0, s)), hspec, hspec,
                  wspec_in, wspec_in, wspec_out, wspec_out,
                  pl.BlockSpec((1, LANES), lambda s: (0, s)), aspec, aspec],
        out_specs=[pl.BlockSpec((ROWS_S, LANES), lambda s: (0, s)), hspec, hspec],
        out_shape=[jax.ShapeDtypeStruct((ROWS_S, S5_WIDTH), F32),
                   jax.ShapeDtypeStruct((DEC_BATCH, n), F32), jax.ShapeDtypeStruct((DEC_BATCH, n), F32)],
        compiler_params=_cp(("parallel",)),
        name="s5_sample",
    )(proj, h0_re.reshape(DEC_BATCH, n), h0_im.reshape(DEC_BATCH, n), bre, bim, cre, cim, d_row, ab_re, ab_im)
    return y, hre.reshape(DEC_BATCH, S5_GROUPS, S5_STATE), him.reshape(DEC_BATCH, S5_GROUPS, S5_STATE)


def _glu_body(y_ref, w_ref, b_ref, o_ref):
    y = y_ref[...]
    z = 0.5 * y * (1.0 + jnp.tanh(math.sqrt(2.0 / math.pi) * (y + 0.044715 * (y * y * y))))
    o_ref[...] = (z * _sigmoid(_dot(z.astype(BF), w_ref[...]) + b_ref[...])).astype(o_ref.dtype)


def _glu(y, w_b, b_row, layer, tm):
    rows = y.shape[0]
    return pl.pallas_call(
        _glu_body,
        grid=(rows // tm,),
        in_specs=[pl.BlockSpec((tm, S5_WIDTH), lambda i: (i, 0)),
                  pl.BlockSpec((None, S5_WIDTH, S5_WIDTH), lambda i: (layer, 0, 0)),
                  pl.BlockSpec((1, S5_WIDTH), lambda i: (0, 0))],
        out_specs=pl.BlockSpec((tm, S5_WIDTH), lambda i: (i, 0)),
        out_shape=jax.ShapeDtypeStruct((rows, S5_WIDTH), BF),
        compiler_params=_cp(("parallel",)),
        name="s5_glu",
    )(y, w_b, b_row)


def _log_sigmoid(x):
    return jnp.minimum(x, 0.0) - jnp.log1p(jnp.exp(-jnp.abs(x)))


def _split_dot(tri_b, x):
    hi = x.astype(BF)
    r1 = x - hi.astype(F32)
    mid = r1.astype(BF)
    lo = (r1 - mid.astype(F32)).astype(BF)
    return _dot(tri_b, hi) + _dot(tri_b, mid) + _dot(tri_b, lo)


def _mlstm_body(*refs, chunk, n_valid, rows_in, has_init):
    q_ref, k_ref, v_ref, og_ref, g_ref, bias_ref, ng_ref = refs[:7]
    pos = 7
    if has_init:
        c0_ref, n0_ref, m0_ref = refs[pos:pos + 3]
        pos += 3
    y_ref, c_ref, n_ref, m_ref = refs[pos:pos + 4]
    pads = refs[pos + 4:]
    c = pl.program_id(1)
    L = chunk

    @pl.when(c == 0)
    def _():
        if has_init:
            c_ref[...] = c0_ref[...]
            n_ref[...] = n0_ref[...]
            m_ref[...] = m0_ref[...]
        else:
            c_ref[...] = jnp.zeros_like(c_ref)
            n_ref[...] = jnp.zeros_like(n_ref)
            m_ref[...] = jnp.zeros_like(m_ref)

    if rows_in < L:
        @pl.when(pl.program_id(0) == 0)
        def _():
            for p in pads:
                p[...] = jnp.zeros_like(p)

        for src, p in zip((q_ref, k_ref, v_ref, og_ref, g_ref), pads):
            p[0:rows_in, :] = src[...]
        q_src, k_src, v_src, og_src, g_src = pads
    else:
        q_src, k_src, v_src, og_src, g_src = q_ref, k_ref, v_ref, og_ref, g_ref

    rid = lax.broadcasted_iota(jnp.int32, (L, 1), 0) + c * L
    valid = rid < n_valid
    lane = lax.broadcasted_iota(jnp.int32, (1, LANES), 1)
    g = g_src[...] + bias_ref[...]
    lf = jnp.where(valid, _log_sigmoid(g), 0.0)
    x = jnp.where(lane < ML_HEADS, jnp.where(valid, g, NEG), lf)
    ti = lax.broadcasted_iota(jnp.int32, (L, L), 0)
    si = lax.broadcasted_iota(jnp.int32, (L, L), 1)
    causal = si <= ti
    tri_b = jnp.where(causal, 1.0, 0.0).astype(BF)
    bc = _split_dot(tri_b, lf)
    xt = x.T
    bt = bc.T
    scale = ML_HEAD_DIM ** -0.5

    for h in range(ML_HEADS):
        sl = slice(h * ML_HEAD_DIM, (h + 1) * ML_HEAD_DIM)
        q = q_src[:, sl]
        k = k_src[:, sl] * scale
        v = v_src[:, sl]
        ig_col = x[:, h:h + 1]
        ig_row = xt[h:h + 1, :]
        b_col = bc[:, ML_HEADS + h:ML_HEADS + h + 1]
        b_row = bt[ML_HEADS + h:ML_HEADS + h + 1, :]
        m_prev = m_ref[h:h + 1, 0:1]
        cmat = c_ref[h]
        n_row = n_ref[h:h + 1, :]

        log_d = jnp.where(causal, b_col - b_row + ig_row, NEG)
        m_inter = b_col + m_prev
        m_t = jnp.maximum(m_inter, jnp.max(log_d, axis=1, keepdims=True))
        dmat = jnp.exp(log_d - m_t)
        w_inter = jnp.exp(m_inter - m_t)
        qb = q.astype(BF)
        kb = k.astype(BF)
        vb = v.astype(BF)
        s = _dot_nt(qb, kb) * dmat
        num = w_inter * _dot_nt(qb, cmat.astype(BF)) + _dot(s.astype(BF), vb)
        den = w_inter * jnp.sum(q * n_row, axis=1, keepdims=True) + jnp.sum(s, axis=1, keepdims=True)
        hh = num / jnp.maximum(jnp.abs(den), jnp.exp(-m_t))
        mu = jnp.mean(hh, axis=1, keepdims=True)
        hc = hh - mu
        var = jnp.mean(hc * hc, axis=1, keepdims=True)
        hn = hc * lax.rsqrt(var + LN_EPS) * ng_ref[:, sl]
        yv = (_sigmoid(og_src[:, sl]) * hn).astype(y_ref.dtype)
        if rows_in < L:
            y_ref[:, sl] = yv[0:rows_in, :]
        else:
            y_ref[:, sl] = yv

        m_new = m_t[L - 1:L, :]
        w_s = jnp.exp(b_col[L - 1:L, :] - b_col + ig_col - m_new)
        f_s = jnp.exp(m_inter[L - 1:L, :] - m_new)
        vw_t = (v * w_s).T
        c_ref[h] = f_s * cmat + _dot(vw_t.astype(BF), kb)
        n_ref[h:h + 1, :] = f_s * n_row + jnp.sum(k * w_s, axis=0, keepdims=True)
        m_ref[h:h + 1, :] = jnp.broadcast_to(m_new, (1, LANES))


def _mlstm_prompt(proj, gates, bias_row, ng_row):
    L = 128
    nc = T_PAD // L
    body = functools.partial(_mlstm_body, chunk=L, n_valid=T_REAL, rows_in=L, has_init=False)
    col = lambda off: pl.BlockSpec((L, ML_WIDTH), lambda b, c: (b * nc + c, off))
    st = lambda *tail: pl.BlockSpec((None,) + tail, lambda b, c: (b,) + (0,) * len(tail))
    y, cst, nst, mst = pl.pallas_call(
        body,
        grid=(BATCH, nc),
        in_specs=[col(1), col(2), col(3), col(4),
                  pl.BlockSpec((L, LANES), lambda b, c: (b * nc + c, 0)),
                  pl.BlockSpec((1, LANES), lambda b, c: (0, 0)),
                  pl.BlockSpec((1, ML_WIDTH), lambda b, c: (0, 0))],
        out_specs=[pl.BlockSpec((L, ML_WIDTH), lambda b, c: (b * nc + c, 0)),
                   st(ML_HEADS, ML_HEAD_DIM, ML_HEAD_DIM), st(ML_HEADS, ML_HEAD_DIM), st(ML_HEADS, LANES)],
        out_shape=[jax.ShapeDtypeStruct((ROWS_P, ML_WIDTH), BF),
                   jax.ShapeDtypeStruct((BATCH, ML_HEADS, ML_HEAD_DIM, ML_HEAD_DIM), F32),
                   jax.ShapeDtypeStruct((BATCH, ML_HEADS, ML_HEAD_DIM), F32),
                   jax.ShapeDtypeStruct((BATCH, ML_HEADS, LANES), F32)],
        compiler_params=_cp(("parallel", "arbitrary")),
        name="mlstm_prompt",
    )(proj, proj, proj, proj, gates, bias_row, ng_row)
    return y, cst, nst, mst[:, :, 0]


def _mlstm_sample(proj_bm, gates_bm, bias_row, ng_row, c0, n0, m0, layer):
    L = 128
    body = functools.partial(_mlstm_body, chunk=L, n_valid=DEC_SEQ, rows_in=DEC_SEQ, has_init=True)
    col = lambda off: pl.BlockSpec((None, DEC_SEQ, ML_WIDTH), lambda b, c: (b, 0, off))
    st_in = lambda *tail: pl.BlockSpec((None, None) + tail, lambda b, c: (layer, b) + (0,) * len(tail))
    st = lambda *tail: pl.BlockSpec((None,) + tail, lambda b, c: (b,) + (0,) * len(tail))
    m0b = jnp.broadcast_to(m0[..., None], m0.shape + (LANES,))
    y, cst, nst, mst = pl.pallas_call(
        body,
        grid=(DEC_BATCH, 1),
        in_specs=[col(1), col(2), col(3), col(4),
                  pl.BlockSpec((None, DEC_SEQ, LANES), lambda b, c: (b, 0, 0)),
                  pl.BlockSpec((1, LANES), lambda b, c: (0, 0)),
                  pl.BlockSpec((1, ML_WIDTH), lambda b, c: (0, 0)),
                  st_in(ML_HEADS, ML_HEAD_DIM, ML_HEAD_DIM), st_in(ML_HEADS, ML_HEAD_DIM), st_in(ML_HEADS, LANES)],
        out_specs=[pl.BlockSpec((None, DEC_SEQ, ML_WIDTH), lambda b, c: (b, 0, 0)),
                   st(ML_HEADS, ML_HEAD_DIM, ML_HEAD_DIM), st(ML_HEADS, ML_HEAD_DIM), st(ML_HEADS, LANES)],
        out_shape=[jax.ShapeDtypeStruct((DEC_BATCH, DEC_SEQ, ML_WIDTH), BF),
                   jax.ShapeDtypeStruct((DEC_BATCH, ML_HEADS, ML_HEAD_DIM, ML_HEAD_DIM), F32),
                   jax.ShapeDtypeStruct((DEC_BATCH, ML_HEADS, ML_HEAD_DIM), F32),
                   jax.ShapeDtypeStruct((DEC_BATCH, ML_HEADS, LANES), F32)],
        scratch_shapes=[pltpu.VMEM((L, ML_WIDTH), F32)] * 4 + [pltpu.VMEM((L, LANES), F32)],
        compiler_params=_cp(("arbitrary", "arbitrary")),
        name="mlstm_sample",
    )(proj_bm, proj_bm, proj_bm, proj_bm, gates_bm, bias_row, ng_row, c0, n0, m0b)
    return y, cst, nst, mst[:, :, 0]


def _proj_ln_body(*refs, n_in):
    a_refs = refs[:n_in]
    w_refs = refs[n_in:2 * n_in]
    x_ref, g_ref, b_ref, o_ref = refs[2 * n_in:]
    y = _dot(a_refs[0][...].astype(BF), w_refs[0][...])
    for a_ref, w_ref in zip(a_refs[1:], w_refs[1:]):
        y = y + _dot(a_ref[...].astype(BF), w_ref[...])
    o_ref[...] = _layer_norm(ALPHA * x_ref[...] + y, g_ref[...], b_ref[...])


def _proj_ln(acts, w_b, layer, x, g_row, b_row, tm):
    rows = x.shape[0]
    n_in = len(acts)
    a_specs, w_specs = [], []
    for a in acts:
        kk = a.shape[1]
        slab = len(w_specs)
        a_specs.append(pl.BlockSpec((tm, kk), lambda i: (i, 0)))
        w_specs.append(pl.BlockSpec((None, kk, D_MODEL), lambda i, slab=slab: (layer, slab, 0)))
    row_spec = pl.BlockSpec((1, D_MODEL), lambda i: (0, 0))
    return pl.pallas_call(
        functools.partial(_proj_ln_body, n_in=n_in),
        grid=(rows // tm,),
        in_specs=a_specs + w_specs + [pl.BlockSpec((tm, D_MODEL), lambda i: (i, 0)), row_spec, row_spec],
        out_specs=pl.BlockSpec((tm, D_MODEL), lambda i: (i, 0)),
        out_shape=jax.ShapeDtypeStruct((rows, D_MODEL), F32),
        compiler_params=_cp(("parallel",)),
        name="proj_ln",
    )(*acts, *([w_b] * n_in), x, g_row, b_row)


HALO = BF16_ROWS


def _conv3(up, cw, cb, lo, n):
    r1 = pltpu.roll(up, 1, 0)
    r2 = pltpu.roll(up, 2, 0)
    return cb + cw[2:3, :] * up[lo:lo + n] + cw[1:2, :] * r1[lo:lo + n] + cw[0:1, :] * r2[lo:lo + n]


def _ffn_prompt_body(x_ref, xh_ref, wv_ref, wg_ref, cwv_ref, cwg_ref, cbv_ref, cbg_ref, wd_ref,
                     g_ref, b_ref, o_ref, xb_s, acc_s, *, tm, tiles_per_seq):
    i = pl.program_id(0)
    j = pl.program_id(1)

    @pl.when(j == 0)
    def _():
        first = (i % tiles_per_seq) == 0
        halo = jnp.where(first, 0.0, xh_ref[...])
        xb_s[0:HALO, :] = halo.astype(BF)
        xb_s[HALO:, :] = x_ref[...].astype(BF)
        acc_s[...] = jnp.zeros_like(acc_s)

    xb = xb_s[...]
    hv = _conv3(_dot(xb, wv_ref[...]), cwv_ref[...], cbv_ref[...], HALO, tm)
    hg = _conv3(_dot(xb, wg_ref[...]), cwg_ref[...], cbg_ref[...], HALO, tm)
    act = hg * _sigmoid(hg) * hv
    acc_s[...] += _dot(act.astype(BF), wd_ref[...])

    @pl.when(j == pl.num_programs(1) - 1)
    def _():
        o_ref[...] = _layer_norm(ALPHA * x_ref[...] + acc_s[...], g_ref[...], b_ref[...])


def _ffn_prompt(x, w_up_b, conv_w, conv_b, w_down_b, layer, g_row, b_row):
    tm, tf = 704, 512
    nf = D_FF // tf
    tiles_per_seq = T_PAD // tm
    body = functools.partial(_ffn_prompt_body, tm=tm, tiles_per_seq=tiles_per_seq)
    halo_blocks = tm // HALO
    row_spec = pl.BlockSpec((1, D_MODEL), lambda i, j: (0, 0))
    return pl.pallas_call(
        body,
        grid=(ROWS_P // tm, nf),
        in_specs=[pl.BlockSpec((tm, D_MODEL), lambda i, j: (i, 0)),
                  pl.BlockSpec((HALO, D_MODEL), lambda i, j: (jnp.maximum(i * halo_blocks - 1, 0), 0)),
                  pl.BlockSpec((None, D_MODEL, tf), lambda i, j: (layer, 0, j)),
                  pl.BlockSpec((None, D_MODEL, tf), lambda i, j: (layer, 0, j + nf)),
                  pl.BlockSpec((None, 3, tf), lambda i, j: (layer, 0, j)),
                  pl.BlockSpec((None, 3, tf), lambda i, j: (layer, 0, j + nf)),
                  pl.BlockSpec((None, 1, tf), lambda i, j: (layer, 0, j)),
                  pl.BlockSpec((None, 1, tf), lambda i, j: (layer, 0, j + nf)),
                  pl.BlockSpec((None, tf, D_MODEL), lambda i, j: (layer, j, 0)),
                  row_spec, row_spec],
        out_specs=pl.BlockSpec((tm, D_MODEL), lambda i, j: (i, 0)),
        out_shape=jax.ShapeDtypeStruct((ROWS_P, D_MODEL), F32),
        scratch_shapes=[pltpu.VMEM((tm + HALO, D_MODEL), BF), pltpu.VMEM((tm, D_MODEL), F32)],
        compiler_params=_cp(("parallel", "arbitrary")),
        name="ffn_prompt",
    )(x, x, w_up_b, w_up_b, conv_w, conv_w, conv_b, conv_b, w_down_b, g_row, b_row)


EXTRA = SUBLANES


def _ffn_sample_body(x_ref, p0v_ref, p0g_ref, p1v_ref, p1g_ref, wv_ref, wg_ref, cwv_ref, cwg_ref,
                     cbv_ref, cbg_ref, wd_ref, g_ref, b_ref, o_ref, cs_ref, cp_ref, xb_s, acc_s):
    j = pl.program_id(0)

    @pl.when(j == 0)
    def _():
        xb_s[...] = x_ref[...].astype(BF)
        acc_s[...] = jnp.zeros_like(acc_s)

    xb = xb_s[...]
    n = ROWS_S
    bsz = DEC_BATCH

    def conv(up, p0_ref, p1_ref, cw_ref, cb_ref):
        cw = cw_ref[...]
        ext = jnp.concatenate([p0_ref[...], p1_ref[...], up[0:n]], axis=0)
        return (cb_ref[...] + cw[0:1, :] * ext[0:n] + cw[1:2, :] * ext[bsz:bsz + n]
                + cw[2:3, :] * ext[2 * bsz:2 * bsz + n])

    upv = _dot(xb, wv_ref[...])
    upg = _dot(xb, wg_ref[...])
    hv = conv(upv, p0v_ref, p1v_ref, cwv_ref, cbv_ref)
    hg = conv(upg, p0g_ref, p1g_ref, cwg_ref, cbg_ref)
    act = hg * _sigmoid(hg) * hv
    acc_s[...] += _dot(act.astype(BF), wd_ref[...])
    lo = (DEC_SEQ - 2) * bsz
    cs_ref[0] = upv[lo:lo + bsz]
    cs_ref[1] = upg[lo:lo + bsz]
    cs_ref[2] = upv[lo + bsz:lo + 2 * bsz]
    cs_ref[3] = upg[lo + bsz:lo + 2 * bsz]
    cp_ref[0] = upv[n:n + EXTRA]
    cp_ref[1] = upg[n:n + EXTRA]

    @pl.when(j == pl.num_programs(0) - 1)
    def _():
        o_ref[...] = _layer_norm(ALPHA * x_ref[0:n, :] + acc_s[...], g_ref[...], b_ref[...])


def _ffn_sample(x_ext, prev, w_up_b, conv_w, conv_b, w_down_b, layer, g_row, b_row):
    tf = 512
    nf = D_FF // tf
    rows = ROWS_S + EXTRA
    prev_spec = lambda q: pl.BlockSpec((None, DEC_BATCH, tf), lambda j: (layer, 0, j + q * nf))
    row_spec = pl.BlockSpec((1, D_MODEL), lambda j: (0, 0))
    return pl.pallas_call(
        _ffn_sample_body,
        grid=(nf,),
        in_specs=[pl.BlockSpec((rows, D_MODEL), lambda j: (0, 0)),
                  prev_spec(0), prev_spec(1), prev_spec(2), prev_spec(3),
                  pl.BlockSpec((None, D_MODEL, tf), lambda j: (layer, 0, j)),
                  pl.BlockSpec((None, D_MODEL, tf), lambda j: (layer, 0, j + nf)),
                  pl.BlockSpec((None, 3, tf), lambda j: (layer, 0, j)),
                  pl.BlockSpec((None, 3, tf), lambda j: (layer, 0, j + nf)),
                  pl.BlockSpec((None, 1, tf), lambda j: (layer, 0, j)),
                  pl.BlockSpec((None, 1, tf), lambda j: (layer, 0, j + nf)),
                  pl.BlockSpec((None, tf, D_MODEL), lambda j: (layer, j, 0)),
                  row_spec, row_spec],
        out_specs=[pl.BlockSpec((ROWS_S, D_MODEL), lambda j: (0, 0)),
                   pl.BlockSpec((4, DEC_BATCH, tf), lambda j: (0, 0, j)),
                   pl.BlockSpec((2, EXTRA, tf), lambda j: (0, 0, j))],
        out_shape=[jax.ShapeDtypeStruct((ROWS_S, D_MODEL), F32),
                   jax.ShapeDtypeStruct((4, DEC_BATCH, D_FF), F32),
                   jax.ShapeDtypeStruct((2, EXTRA, D_FF), F32)],
        scratch_shapes=[pltpu.VMEM((rows, D_MODEL), BF), pltpu.VMEM((ROWS_S, D_MODEL), F32)],
        compiler_params=_cp(("arbitrary",)),
        name="ffn_sample",
    )(x_ext, prev, prev, prev, prev, w_up_b, w_up_b, conv_w, conv_w, conv_b, conv_b, w_down_b, g_row, b_row)


def _rms(xf, g):
    return xf * lax.rsqrt(jnp.mean(xf * xf, axis=-1, keepdims=True) + RMS_EPS) * g


def _mla_in_body(x_ref, win_ref, gq_ref, gkv_ref, wuq_ref, cos_ref, sin_ref,
                 ckv_ref, kk_ref, qn_ref, qp_ref):
    p = _dot(x_ref[...].astype(BF), win_ref[...])
    cq = _rms(p[:, 0:Q_LORA], gq_ref[...])
    ckv_ref[...] = _rms(p[:, Q_LORA:Q_LORA + KV_LORA], gkv_ref[...])
    cos = cos_ref[...]
    sin = sin_ref[...]
    o = Q_LORA + KV_LORA
    kk_ref[...] = p[:, o:o + LANES] * cos + p[:, o + LANES:o + 2 * LANES] * sin
    q = _dot(cq.astype(BF), wuq_ref[...])
    n_nope = MLA_HEADS * QK_NOPE
    n_pe = MLA_HEADS * QK_ROPE
    qn_ref[...] = q[:, 0:n_nope].astype(BF)
    reps = n_pe // LANES
    cos_w = jnp.concatenate([cos] * reps, axis=1)
    sin_w = jnp.concatenate([sin] * reps, axis=1)
    qp_ref[...] = (q[:, n_nope:n_nope + n_pe] * cos_w + q[:, n_nope + n_pe:] * sin_w).astype(BF)


def _mla_in(x, win_b, gq_row, gkv_row, wuq_b, cos_t, sin_t, layer, tm):
    rows = x.shape[0]
    n_in = win_b.shape[-1]
    n_q = wuq_b.shape[-1]
    row = lambda w: pl.BlockSpec((tm, w), lambda i: (i, 0))
    return pl.pallas_call(
        _mla_in_body,
        grid=(rows // tm,),
        in_specs=[row(D_MODEL),
                  pl.BlockSpec((None, D_MODEL, n_in), lambda i: (layer, 0, 0)),
                  pl.BlockSpec((1, Q_LORA), lambda i: (0, 0)),
                  pl.BlockSpec((1, KV_LORA), lambda i: (0, 0)),
                  pl.BlockSpec((None, Q_LORA, n_q), lambda i: (layer, 0, 0)),
                  row(LANES), row(LANES)],
        out_specs=[row(KV_LORA), row(LANES), row(MLA_HEADS * QK_NOPE), row(MLA_HEADS * QK_ROPE)],
        out_shape=[jax.ShapeDtypeStruct((rows, KV_LORA), F32),
                   jax.ShapeDtypeStruct((rows, LANES), F32),
                   jax.ShapeDtypeStruct((rows, MLA_HEADS * QK_NOPE), BF),
                   jax.ShapeDtypeStruct((rows, MLA_HEADS * QK_ROPE), BF)],
        compiler_params=_cp(("parallel",)),
        name="mla_in",
    )(x, win_b, gq_row, gkv_row, wuq_b, cos_t, sin_t)


def _fold(x, op):
    out = x[:, 0:LANES]
    for c in range(1, x.shape[1] // LANES):
        out = op(out, x[:, c * LANES:(c + 1) * LANES])
    return out


def _flash_body(qn_ref, qp_ref, kn_ref, kk_ref, v_ref, o_ref, s_s, mx_s, l_s, acc_s, *, tq):
    qi = pl.program_id(2)
    lane = lax.broadcasted_iota(jnp.int32, (1, LANES), 1)
    qp = qp_ref[...].astype(F32)
    qn = qn_ref[...]
    half = LANES // 2
    qs = (jnp.concatenate([qn[:, 0:QK_NOPE], jnp.where(lane < half, qp, 0.0).astype(BF)], axis=1),
          jnp.concatenate([qn[:, QK_NOPE:], jnp.where(lane >= half, qp, 0.0).astype(BF)], axis=1))
    mx_s[...] = jnp.full_like(mx_s, NEG)
    l_s[...] = jnp.zeros_like(l_s)
    acc_s[...] = jnp.zeros_like(acc_s)
    reps = tq // LANES

    def logits(kc, masked):
        off = pl.multiple_of(kc * tq, tq)
        kn = kn_ref[pl.ds(off, tq), :]
        kk = kk_ref[pl.ds(off, tq), :].astype(BF)
        for hh in range(2):
            kh = jnp.concatenate([kn[:, hh * QK_NOPE:(hh + 1) * QK_NOPE], kk], axis=1)
            s = _dot_nt(qs[hh], kh) * (MLA_SCALE * LOG2E)
            if masked:
                ti = lax.broadcasted_iota(jnp.int32, (tq, tq), 0)
                si = lax.broadcasted_iota(jnp.int32, (tq, tq), 1)
                s = jnp.where(si <= ti, s, NEG)
            s_s[hh, kc] = s
            mx_s[hh] = jnp.maximum(mx_s[hh], _fold(s, jnp.maximum))

    def pairs(n, fn):
        def two(p, carry):
            fn(2 * p)
            fn(2 * p + 1)
            return carry

        lax.fori_loop(0, n // 2, two, 0)

        @pl.when(n % 2 == 1)
        def _():
            fn(n - 1)

    pairs(qi, lambda kc: logits(kc, False))
    logits(qi, True)

    m_rows = [jnp.concatenate([jnp.broadcast_to(jnp.max(mx_s[hh], axis=1, keepdims=True), (tq, LANES))] * reps,
                              axis=1) for hh in range(2)]

    def weighted(kc):
        off = pl.multiple_of(kc * tq, tq)
        vv = v_ref[pl.ds(off, tq), :]
        for hh in range(2):
            p = jnp.exp2(s_s[hh, kc] - m_rows[hh])
            l_s[hh] += _fold(p, jnp.add)
            acc_s[hh] += _dot(p.astype(BF), vv[:, hh * V_DIM:(hh + 1) * V_DIM])

    pairs(qi + 1, weighted)
    outs = [acc_s[hh] / jnp.sum(l_s[hh], axis=1, keepdims=True) for hh in range(2)]
    o_ref[...] = jnp.concatenate(outs, axis=1).astype(o_ref.dtype)


def _flash(qn, qp, kn, kk, v):
    tq = 384
    nq = T_PAD // tq
    body = functools.partial(_flash_body, tq=tq)
    two = 2 * QK_NOPE
    return pl.pallas_call(
        body,
        grid=(BATCH, MLA_HEADS // 2, nq),
        in_specs=[pl.BlockSpec((tq, two), lambda b, h, q: (b * nq + q, h)),
                  pl.BlockSpec((tq, LANES), lambda b, h, q: (b * nq + q, h)),
                  pl.BlockSpec((T_PAD, two), lambda b, h, q: (b, h)),
                  pl.BlockSpec((T_PAD, LANES), lambda b, h, q: (b, 0)),
                  pl.BlockSpec((T_PAD, two), lambda b, h, q: (b, h))],
        out_specs=pl.BlockSpec((tq, two), lambda b, h, q: (b * nq + q, h)),
        out_shape=jax.ShapeDtypeStruct((ROWS_P, MLA_HEADS * V_DIM), BF),
        scratch_shapes=[pltpu.VMEM((2, nq, tq, tq), F32), pltpu.VMEM((2, tq, LANES), F32),
                        pltpu.VMEM((2, tq, LANES), F32), pltpu.VMEM((2, tq, V_DIM), F32)],
        compiler_params=_cp(("parallel", "parallel", "arbitrary")),
        name="mla_flash",
    )(qn, qp, kn, kk, v)


def _absorb_body(qn_ref, wuk_ref, o_ref):
    o_ref[...] = _dot_nt(qn_ref[...], wuk_ref[...]).astype(o_ref.dtype)


def _absorb(qn, wuk_b, layer):
    return pl.pallas_call(
        _absorb_body,
        grid=(MLA_HEADS,),
        in_specs=[pl.BlockSpec((ROWS_S, QK_NOPE), lambda h: (0, h)),
                  pl.BlockSpec((None, KV_LORA, QK_NOPE), lambda h: (layer, 0, h))],
        out_specs=pl.BlockSpec((ROWS_S, KV_LORA), lambda h: (0, h)),
        out_shape=jax.ShapeDtypeStruct((ROWS_S, MLA_HEADS * KV_LORA), BF),
        compiler_params=_cp(("parallel",)),
        name="mla_absorb",
    )(qn, wuk_b)


def _unabsorb_body(ol_ref, wuv_ref, o_ref):
    o_ref[...] = _dot(ol_ref[...], wuv_ref[...]).astype(o_ref.dtype)


def _unabsorb(ol, wuv_b, layer):
    return pl.pallas_call(
        _unabsorb_body,
        grid=(MLA_HEADS,),
        in_specs=[pl.BlockSpec((ROWS_S, KV_LORA), lambda h: (0, h)),
                  pl.BlockSpec((None, KV_LORA, V_DIM), lambda h: (layer, 0, h))],
        out_specs=pl.BlockSpec((ROWS_S, V_DIM), lambda h: (0, h)),
        out_shape=jax.ShapeDtypeStruct((ROWS_S, MLA_HEADS * V_DIM), BF),
        compiler_params=_cp(("parallel",)),
        name="mla_unabsorb",
    )(ol, wuv_b)


QROWS = DEC_SEQ * MLA_HEADS
AHEAD = 3


def _paged_body(pt_ref, ql_ref, qp_ref, cn_ref, kn_ref, ckv_hbm, kpe_hbm, o_ref,
                cbuf, kbuf, sem, m_s, l_s, acc_s, newc_s, newk_s, *, layer, n_chunks, n_pages):
    b = pl.program_id(0)
    page = LANES

    def chunk_copies(seq, c, slot, known_pages):
        out = []
        for i in range(n_pages):
            pid = pt_ref[seq, c * n_pages + i] if known_pages else 0
            out.append(pltpu.make_async_copy(ckv_hbm.at[layer, pid],
                                             cbuf.at[slot, pl.ds(i * page, page), :], sem.at[slot]))
            out.append(pltpu.make_async_copy(kpe_hbm.at[layer, pid],
                                             kbuf.at[slot, :, pl.ds(i * page, page)], sem.at[slot]))
        return out

    @pl.when(b == 0)
    def _():
        for c in range(AHEAD):
            for cp in chunk_copies(0, c, c, True):
                cp.start()

    m_s[...] = jnp.full_like(m_s, NEG)
    l_s[...] = jnp.zeros_like(l_s)
    acc_s[...] = jnp.zeros_like(acc_s)

    ql = ql_ref[...].reshape(QROWS, KV_LORA)
    qp = qp_ref[...].reshape(QROWS, QK_ROPE)

    def local_softmax(s, vb):
        m = jnp.max(s, axis=1, keepdims=True)
        p = jnp.exp(s - m)
        return m, jnp.sum(p, axis=1, keepdims=True), _dot(p.astype(BF), vb)

    def merge(parts):
        m_prev = m_s[...]
        m_new = m_prev
        for m, _, _ in parts:
            m_new = jnp.maximum(m_new, m)
        alpha = jnp.exp(m_prev - m_new)
        l = alpha * l_s[...]
        acc = alpha * acc_s[...]
        for m, lg, ag in parts:
            w = jnp.exp(m - m_new)
            l = l + w * lg
            acc = acc + w * ag
        m_s[...] = m_new
        l_s[...] = l
        acc_s[...] = acc

    for c in range(n_chunks):
        for cp in chunk_copies(b, c, c, False):
            cp.wait()
        nxt = c + AHEAD
        if nxt < n_chunks:
            for cp in chunk_copies(b, nxt, nxt, True):
                cp.start()
        else:
            @pl.when(b + 1 < pl.num_programs(0))
            def _():
                for cp in chunk_copies(b + 1, nxt - n_chunks, nxt - n_chunks, True):
                    cp.start()
        cb = cbuf[c].astype(BF)
        kt = kbuf[c].astype(BF)
        merge([local_softmax((_dot_nt(ql, cb) + _dot(qp, kt)) * MLA_SCALE, cb)])

    newc_s[...] = jnp.zeros_like(newc_s)
    newk_s[...] = jnp.zeros_like(newk_s)
    newc_s[0:DEC_SEQ, :] = cn_ref[...]
    newk_s[0:DEC_SEQ, :] = kn_ref[...]
    cn = newc_s[...].astype(BF)
    s = (_dot_nt(ql, cn) + _dot_nt(qp, newk_s[...].astype(BF))) * MLA_SCALE
    qt = lax.broadcasted_iota(jnp.int32, (QROWS, LANES), 0) // MLA_HEADS
    kt = lax.broadcasted_iota(jnp.int32, (QROWS, LANES), 1)
    merge([local_softmax(jnp.where(kt <= qt, s, NEG), cn)])
    o = acc_s[...] / l_s[...]
    o_ref[...] = o.reshape(DEC_SEQ, MLA_HEADS, KV_LORA).astype(o_ref.dtype)


def _paged_attention(page_table, q_lat, q_pe, ckv_new, kpe_new, cache_ckv, cache_kpe_t, layer):
    n_pages = PAGES_PER_STEP
    n_chunks = page_table.shape[1] // n_pages
    assert n_chunks > AHEAD and cache_ckv.shape[2] == LANES
    keys = n_pages * LANES
    body = functools.partial(_paged_body, layer=layer, n_chunks=n_chunks, n_pages=n_pages)
    qspec = lambda w: pl.BlockSpec((DEC_SEQ, None, MLA_HEADS, w), lambda b, pt: (0, b, 0, 0))
    nspec = lambda w: pl.BlockSpec((None, DEC_SEQ, w), lambda b, pt: (b, 0, 0))
    grid_spec = pltpu.PrefetchScalarGridSpec(
        num_scalar_prefetch=1,
        grid=(DEC_BATCH,),
        in_specs=[qspec(KV_LORA), qspec(QK_ROPE), nspec(KV_LORA), nspec(QK_ROPE),
                  pl.BlockSpec(memory_space=pl.ANY), pl.BlockSpec(memory_space=pl.ANY)],
        out_specs=pl.BlockSpec((DEC_SEQ, None, MLA_HEADS, KV_LORA), lambda b, pt: (0, b, 0, 0)),
        scratch_shapes=[pltpu.VMEM((n_chunks, keys, KV_LORA), F32), pltpu.VMEM((n_chunks, QK_ROPE, keys), F32),
                        pltpu.SemaphoreType.DMA((n_chunks,)),
                        pltpu.VMEM((QROWS, 1), F32), pltpu.VMEM((QROWS, 1), F32),
                        pltpu.VMEM((QROWS, KV_LORA), F32),
                        pltpu.VMEM((LANES, KV_LORA), F32), pltpu.VMEM((LANES, QK_ROPE), F32)],
    )
    return pl.pallas_call(
        body,
        grid_spec=grid_spec,
        out_shape=jax.ShapeDtypeStruct((DEC_SEQ, DEC_BATCH, MLA_HEADS, KV_LORA), BF),
        compiler_params=_cp(("arbitrary",)),
        name="mla_paged",
    )(page_table, q_lat, q_pe, ckv_new, kpe_new, cache_ckv, cache_kpe_t)


def _rope_tables(pos):
    half = QK_ROPE // 2
    inv = ROPE_THETA ** (-jnp.arange(half, dtype=F32) / half)
    ang = pos.astype(F32)[:, None] * inv[None, :]
    cos, sin = jnp.cos(ang), jnp.sin(ang)
    return (jnp.concatenate([cos, cos, cos, cos], axis=1),
            jnp.concatenate([-sin, sin, -sin, sin], axis=1))


def _swap_halves(w):
    half = w.shape[-1] // 2
    return jnp.concatenate([w[..., half:], w[..., :half]], axis=-1)


def kernel(x_prompt, x_sample, cache_mla_ckv, cache_mla_kpe, page_table, state_s5_re, state_s5_im, state_mlstm_c, state_mlstm_n, state_mlstm_m, state_ffn_conv, meta_tokens, ln1_g, ln1_b, ln2_g, ln2_b, mix_w_in, mix_b_gates, s5_a_re, s5_a_im, s5_log_dt, s5_b_re, s5_b_im, s5_c_re, s5_c_im, s5_d, s5_w_glu, s5_b_glu, ml_norm_g, mix_w_out, mla_w_in, mla_q_norm_g, mla_kv_norm_g, mla_w_uq, mla_w_uk, mla_w_uv, mla_w_out, ffn_w_up, ffn_conv_w, ffn_conv_b, ffn_w_down):
    n_mix = mix_w_in.shape[0]
    n_mla = mla_w_in.shape[0]
    past = page_table.shape[1] * cache_mla_ckv.shape[2]

    n_main = S5_WIDTH + 4 * ML_WIDTH
    mix_w_in_b = mix_w_in.astype(BF)
    mix_w_g_b = jnp.pad(mix_w_in[:, :, n_main:], ((0, 0), (0, 0), (0, LANES - 2 * ML_HEADS))).astype(BF)
    gate_bias = jnp.pad(mix_b_gates, ((0, 0), (0, LANES - 2 * ML_HEADS)))
    w_glu_b = s5_w_glu.astype(BF)
    mix_w_out_b = mix_w_out.astype(BF)
    o = Q_LORA + KV_LORA
    w_kpe = mla_w_in[:, :, o:]
    w_kpe_sw = _swap_halves(w_kpe)
    mla_w_in_b = jnp.concatenate([mla_w_in[:, :, :o], w_kpe, w_kpe, w_kpe_sw, w_kpe_sw], axis=-1).astype(BF)
    wq4 = mla_w_uq.reshape(n_mla, Q_LORA, MLA_HEADS, QK_NOPE + QK_ROPE)
    wq_pe = wq4[..., QK_NOPE:]
    mla_w_uq_b = jnp.concatenate(
        [wq4[..., :QK_NOPE].reshape(n_mla, Q_LORA, -1), wq_pe.reshape(n_mla, Q_LORA, -1),
         _swap_halves(wq_pe).reshape(n_mla, Q_LORA, -1)], axis=-1).astype(BF)
    mla_w_uk_b = mla_w_uk.reshape(n_mla, KV_LORA, MLA_HEADS * QK_NOPE).astype(BF)
    mla_w_uv_b = mla_w_uv.reshape(n_mla, KV_LORA, MLA_HEADS * V_DIM).astype(BF)
    mla_w_out_b = mla_w_out.astype(BF)
    ffn_w_up_b = ffn_w_up.astype(BF)
    ffn_w_down_b = ffn_w_down.astype(BF)
    conv_b3 = ffn_conv_b[:, None, :]
    prev_conv = state_ffn_conv.reshape(DEPTH, DEC_BATCH, 4 * D_FF)
    cache_kpe_t = jnp.swapaxes(cache_mla_kpe, 2, 3)
    cos_p, sin_p = _rope_tables(jnp.arange(T_PAD))
    cos_p = jnp.concatenate([cos_p] * BATCH, axis=0)
    sin_p = jnp.concatenate([sin_p] * BATCH, axis=0)
    cos_s, sin_s = _rope_tables(past + jnp.arange(DEC_SEQ))
    cos_s = jnp.repeat(cos_s, DEC_BATCH, axis=0)
    sin_s = jnp.repeat(sin_s, DEC_BATCH, axis=0)

    xp = jnp.concatenate([jnp.broadcast_to(meta_tokens[None], (BATCH, N_META, D_MODEL)), x_prompt], axis=1)
    xp = jnp.pad(xp, ((0, 0), (0, T_PAD - T_REAL), (0, 0))).reshape(ROWS_P, D_MODEL)
    xs = x_sample.transpose(1, 0, 2).reshape(ROWS_S, D_MODEL)
    tm_p, tm_s = 704, ROWS_S

    outs = {k: [] for k in ("ckv_p", "kpe_p", "ckv_s", "kpe_s", "s5r_p", "s5i_p", "s5r_s", "s5i_s",
                            "mc_p", "mn_p", "mm_p", "mc_s", "mn_s", "mm_s", "conv_p", "conv_s")}

    for l in range(DEPTH):
        j = l // 2
        g1, b1 = ln1_g[l][None, :], ln1_b[l][None, :]
        g2, b2 = ln2_g[l][None, :], ln2_b[l][None, :]
        if l % 2 == 0:
            ab_re, ab_im, bb_re, bb_im = _s5_params(s5_a_re[j], s5_a_im[j], s5_log_dt[j], s5_b_re[j], s5_b_im[j])
            bre, bim = _blockdiag_in(bb_re), _blockdiag_in(bb_im)
            cre, cim = _blockdiag_out(s5_c_re[j]), _blockdiag_out(s5_c_im[j])
            d_row = s5_d[j].reshape(1, S5_WIDTH)
            bias_row = gate_bias[j][None, :]
            ng_row = ml_norm_g[j][None, :]
            bglu_row = s5_b_glu[j][None, :]

            proj, gates = _mix_proj(xp, mix_w_in_b, mix_w_g_b, j, tm_p)
            y5, hr, hi = _s5_prompt(proj, bre, bim, cre, cim, d_row, ab_re, ab_im)
            y5 = _glu(y5, w_glu_b, bglu_row, j, tm_p)
            yml, c1, n1, m1 = _mlstm_prompt(proj, gates, bias_row, ng_row)
            xp = _proj_ln([y5, yml], mix_w_out_b, j, xp, g1, b1, 352)
            outs["s5r_p"].append(hr); outs["s5i_p"].append(hi)
            outs["mc_p"].append(c1); outs["mn_p"].append(n1); outs["mm_p"].append(m1)

            proj, gates = _mix_proj(xs, mix_w_in_b, mix_w_g_b, j, tm_s)
            y5, hr, hi = _s5_sample(proj, state_s5_re[j], state_s5_im[j], bre, bim, cre, cim, d_row, ab_re, ab_im)
            y5 = _glu(y5, w_glu_b, bglu_row, j, tm_s)
            proj_bm = proj.reshape(DEC_SEQ, DEC_BATCH, n_main).transpose(1, 0, 2)
            gates_bm = gates.reshape(DEC_SEQ, DEC_BATCH, LANES).transpose(1, 0, 2)
            yml, c1, n1, m1 = _mlstm_sample(proj_bm, gates_bm, bias_row, ng_row,
                                            state_mlstm_c, state_mlstm_n, state_mlstm_m, j)
            yml = yml.transpose(1, 0, 2).reshape(ROWS_S, ML_WIDTH)
            xs = _proj_ln([y5, yml], mix_w_out_b, j, xs, g1, b1, 256)
            outs["s5r_s"].append(hr); outs["s5i_s"].append(hi)
            outs["mc_s"].append(c1); outs["mn_s"].append(n1); outs["mm_s"].append(m1)
        else:
            gq_row = mla_q_norm_g[j][None, :]
            gkv_row = mla_kv_norm_g[j][None, :]

            ckv, kk, qn, qp = _mla_in(xp, mla_w_in_b, gq_row, gkv_row, mla_w_uq_b, cos_p, sin_p, j, 352)
            kn = _mm(ckv, mla_w_uk_b, j, BF, 384, MLA_HEADS * QK_NOPE)
            vv = _mm(ckv, mla_w_uv_b, j, BF, 384, MLA_HEADS * V_DIM)
            att = _flash(qn, qp, kn, kk, vv)
            xp = _proj_ln([att], mla_w_out_b, j, xp, g1, b1, 352)
            outs["ckv_p"].append(ckv.reshape(BATCH, T_PAD, KV_LORA)[:, :T_REAL])
            outs["kpe_p"].append(kk.reshape(BATCH, T_PAD, LANES)[:, :T_REAL, :QK_ROPE])

            ckv, kk, qn, qp = _mla_in(xs, mla_w_in_b, gq_row, gkv_row, mla_w_uq_b, cos_s, sin_s, j, 256)
            q_lat = _absorb(qn, mla_w_uk_b, j)
            ckv_bm = ckv.reshape(DEC_SEQ, DEC_BATCH, KV_LORA).transpose(1, 0, 2)
            kpe_bm = kk[:, :QK_ROPE].reshape(DEC_SEQ, DEC_BATCH, QK_ROPE).transpose(1, 0, 2)
            o_lat = _paged_attention(
                page_table,
                q_lat.reshape(DEC_SEQ, DEC_BATCH, MLA_HEADS, KV_LORA),
                qp.reshape(DEC_SEQ, DEC_BATCH, MLA_HEADS, QK_ROPE),
                ckv_bm, kpe_bm, cache_mla_ckv, cache_kpe_t, j)
            att = _unabsorb(o_lat.reshape(ROWS_S, MLA_HEADS * KV_LORA), mla_w_uv_b, j)
            xs = _proj_ln([att], mla_w_out_b, j, xs, g1, b1, 256)
            outs["ckv_s"].append(ckv_bm)
            outs["kpe_s"].append(kpe_bm)

        tail = jnp.concatenate([xp[b * T_PAD + T_REAL - 2:b * T_PAD + T_REAL] for b in range(BATCH)]
                               + [jnp.zeros((EXTRA - 2 * BATCH, D_MODEL), F32)], axis=0)
        xs_ext = jnp.concatenate([xs, tail], axis=0)
        xs, cs, cpv = _ffn_sample(xs_ext, prev_conv, ffn_w_up_b, ffn_conv_w, conv_b3, ffn_w_down_b, l, g2, b2)
        xp = _ffn_prompt(xp, ffn_w_up_b, ffn_conv_w, conv_b3, ffn_w_down_b, l, g2, b2)
        outs["conv_s"].append(cs.transpose(1, 0, 2).reshape(DEC_BATCH, 2, 2 * D_FF))
        cp2 = jnp.concatenate([cpv[0, :2 * BATCH], cpv[1, :2 * BATCH]], axis=-1)
        outs["conv_p"].append(cp2.reshape(BATCH, 2, 2 * D_FF))

    st = lambda k: jnp.stack(outs[k])
    y_prompt = xp.reshape(BATCH, T_PAD, D_MODEL)[:, N_META:T_REAL]
    y_sample = xs.reshape(DEC_SEQ, DEC_BATCH, D_MODEL).transpose(1, 0, 2)
    return (y_prompt, y_sample,
            st("ckv_p"), st("kpe_p"), st("ckv_s"), st("kpe_s"),
            st("s5r_p"), st("s5i_p"), st("s5r_s"), st("s5i_s"),
            st("mc_p"), st("mn_p"), st("mm_p"), st("mc_s"), st("mn_s"), st("mm_s"),
            st("conv_p"), st("conv_s"))
```
